```python
import math
import jax
import jax.numpy as jnp
from jax import lax
import numpy as np


D_MODEL = 1024
BATCH = 4
SEQ = 8192
DEPTH = 1

D_MIX = D_MODEL
D_RNN = D_MIX // 2
RG_BLOCKS = 8
RG_BLOCK_W = D_RNN // RG_BLOCKS
CONV_W = 4
RG_C = 8.0
N_HEADS = 8
N_KV_HEADS = 2
GQA = N_HEADS // N_KV_HEADS
HEAD_DIM = (D_MIX - D_RNN) // N_HEADS
CMP_STRIDE = 16
CMP_LEN = 2 * CMP_STRIDE
CMP_HIDDEN = 2 * HEAD_DIM
SEL_BLOCK = 64
SEL_TOPN = 16
WINDOW = 512
Q_BLOCK = 128
NUM_BUCKETS = 32
MAX_DISTANCE = 128
N_EXPERTS = 256
TOP_K = 8
N_GROUPS = 8
TOPK_GROUPS = 4
D_EXPERT = 256
D_SHARED = 256
ROUTED_SCALE = 2.5
MOE_BLOCK = 128
ALPHA = (2 * DEPTH) ** 0.25
BETA = (8 * DEPTH) ** -0.25
LN_EPS = 1e-5
NEG_INF = -1e30
FORCE_SCORE = 1e9
PROJ_SIZES = (D_RNN, D_RNN, N_HEADS * HEAD_DIM) + (N_KV_HEADS * HEAD_DIM,) * 6 + (3 * N_HEADS,)
N_PROJ = sum(PROJ_SIZES)

kernel_name = 'hybrid_rglru_nsa_moe_block'


def layer_norm(x, g, b):
    xf = x.astype(jnp.float32)
    mu = jnp.mean(xf, axis=-1, keepdims=True)
    xc = xf - mu
    var = jnp.mean(xc * xc, axis=-1, keepdims=True)
    return (xc * lax.rsqrt(var + LN_EPS) * g.astype(jnp.float32) + b.astype(jnp.float32)).astype(x.dtype)


def t5_bucket(dist):
    n = jnp.maximum(dist, 0)
    max_exact = NUM_BUCKETS // 2
    nf = jnp.maximum(n, 1).astype(jnp.float32)
    large = max_exact + (jnp.log(nf / max_exact) / math.log(MAX_DISTANCE / max_exact) * (NUM_BUCKETS - max_exact)).astype(jnp.int32)
    return jnp.where(n < max_exact, n, jnp.minimum(large, NUM_BUCKETS - 1))


def rglru_group(xr, gr, conv_w, conv_b, w_a, b_a, w_x, b_x, lam):
    f32 = jnp.float32
    xc = lax.conv_general_dilated(xr, conv_w[:, None, :], window_strides=(1,), padding=[(CONV_W - 1, 0)],
                                  dimension_numbers=('NWC', 'WIO', 'NWC'), feature_group_count=D_RNN)
    xf = (xc + conv_b).astype(f32)
    xb = xf.reshape(xf.shape[:-1] + (RG_BLOCKS, RG_BLOCK_W))
    r = jax.nn.sigmoid(jnp.einsum('bsnc,ncd->bsnd', xb, w_a.astype(f32)).reshape(xf.shape) + b_a.astype(f32))
    i = jax.nn.sigmoid(jnp.einsum('bsnc,ncd->bsnd', xb, w_x.astype(f32)).reshape(xf.shape) + b_x.astype(f32))
    log_a = -RG_C * r * jax.nn.softplus(-lam.astype(f32))
    a = jnp.exp(log_a)
    u = jnp.sqrt(-jnp.expm1(2.0 * log_a)) * (i * xf)

    def combine(left, right):
        a1, b1 = left
        a2, b2 = right
        return a1 * a2, a2 * b1 + b2

    _, h = lax.associative_scan(combine, (a, u), axis=1)
    return h * jax.nn.gelu(gr.astype(f32))


def compress_kv(kv, pos, w1, b1, w2, b2):
    B, S, hk, dh = kv.shape
    chunks = kv.reshape(B, S // CMP_STRIDE, CMP_STRIDE, hk, dh)
    blocks = jnp.concatenate([chunks[:, :-1], chunks[:, 1:]], axis=2) + pos[:, None, :]
    flat = blocks.transpose(0, 1, 3, 2, 4).reshape(B, -1, hk, CMP_LEN * dh)
    return jax.nn.gelu(flat @ w1 + b1) @ w2 + b2


def nsa_group(q, k_c, v_c, k_s, v_s, k_w, v_w, gates, rel_bias):
    f32 = jnp.float32
    B, S = q.shape[:2]
    n_c = k_c.shape[1]
    n_sb = S // SEL_BLOCK
    n_top = min(SEL_TOPN, n_sb)
    n_qb = S // Q_BLOCK
    q = q.astype(f32) * (HEAD_DIM ** -0.5)
    kc_t = k_c.astype(f32).transpose(0, 2, 1, 3)
    vc_t = v_c.astype(f32).transpose(0, 2, 1, 3)
    ks_blk = k_s.astype(f32).reshape(B, n_sb, SEL_BLOCK, N_KV_HEADS, HEAD_DIM).transpose(0, 3, 1, 2, 4)
    vs_blk = v_s.astype(f32).reshape(B, n_sb, SEL_BLOCK, N_KV_HEADS, HEAD_DIM).transpose(0, 3, 1, 2, 4)
    kw_pad = jnp.pad(k_w.astype(f32), ((0, 0), (WINDOW, 0), (0, 0), (0, 0)))
    vw_pad = jnp.pad(v_w.astype(f32), ((0, 0), (WINDOW, 0), (0, 0), (0, 0)))
    cmp_end = jnp.arange(n_c) * CMP_STRIDE + CMP_LEN - 1
    c_start = jnp.arange(n_c)[:, None] * CMP_STRIDE
    s_start = jnp.arange(n_sb)[None, :] * SEL_BLOCK
    overlap = ((c_start < s_start + SEL_BLOCK) & (c_start + CMP_LEN > s_start)).astype(f32)
    bias_f = rel_bias.astype(f32)
    table_hg = bias_f.T.reshape(N_KV_HEADS, GQA, NUM_BUCKETS)
    h_ix = jnp.arange(N_KV_HEADS)[None, :, None, None, None]
    g_ix = jnp.arange(GQA)[None, None, :, None, None]
    blk = jnp.arange(n_sb)
    gather_blocks = jax.vmap(jax.vmap(lambda blocks, ix: blocks[ix]))

    def head_bias(dist):
        return bias_f[t5_bucket(dist)].transpose(2, 0, 1).reshape((N_KV_HEADS, GQA) + dist.shape)

    def block_fn(qb):
        q0 = qb * Q_BLOCK
        t = q0 + jnp.arange(Q_BLOCK)
        qblk = lax.dynamic_slice_in_dim(q, q0, Q_BLOCK, axis=1).reshape(B, Q_BLOCK, N_KV_HEADS, GQA, HEAD_DIM)
        s_c = jnp.einsum('bqhgd,bhkd->bhgqk', qblk, kc_t) + head_bias(t[:, None] - cmp_end[None, :])
        valid_c = cmp_end[None, :] <= t[:, None]
        p_c = jax.nn.softmax(jnp.where(valid_c, s_c, NEG_INF), axis=-1) * jnp.any(valid_c, axis=-1)[:, None].astype(f32)
        o_c = jnp.einsum('bhgqk,bhkd->bhgqd', p_c, vc_t)
        imp = jnp.einsum('bhgqk,ks->bhqs', p_c, overlap)
        cur = t // SEL_BLOCK
        sel_valid = blk[None, :] <= cur[:, None]
        forced = (blk[None, :] == 0) | (blk[None, :] == cur[:, None]) | (blk[None, :] == cur[:, None] - 1)
        imp = jnp.where(forced, FORCE_SCORE, jnp.where(sel_valid, imp, -1.0))
        _, idx = lax.top_k(imp, n_top)
        k_sel = gather_blocks(ks_blk, idx).reshape(B, N_KV_HEADS, Q_BLOCK, n_top * SEL_BLOCK, HEAD_DIM)
        v_sel = gather_blocks(vs_blk, idx).reshape(B, N_KV_HEADS, Q_BLOCK, n_top * SEL_BLOCK, HEAD_DIM)
        pos = (idx[..., None] * SEL_BLOCK + jnp.arange(SEL_BLOCK)).reshape(B, N_KV_HEADS, Q_BLOCK, n_top * SEL_BLOCK)
        dist_s = t[None, None, :, None] - pos
        s_s = jnp.einsum('bqhgd,bhqkd->bhgqk', qblk, k_sel) + table_hg[h_ix, g_ix, t5_bucket(dist_s)[:, :, None]]
        p_s = jax.nn.softmax(jnp.where((dist_s >= 0)[:, :, None], s_s, NEG_INF), axis=-1)
        o_s = jnp.einsum('bhgqk,bhqkd->bhgqd', p_s, v_sel)
        k_win = lax.dynamic_slice_in_dim(kw_pad, q0, WINDOW + Q_BLOCK, axis=1)
        v_win = lax.dynamic_slice_in_dim(vw_pad, q0, WINDOW + Q_BLOCK, axis=1)
        s_pos = q0 - WINDOW + jnp.arange(WINDOW + Q_BLOCK)
        dist_w = t[:, None] - s_pos[None, :]
        mask_w = (dist_w >= 0) & (dist_w < WINDOW) & (s_pos[None, :] >= 0)
        s_w = jnp.einsum('bqhgd,bkhd->bhgqk', qblk, k_win) + head_bias(dist_w)
        p_w = jax.nn.softmax(jnp.where(mask_w, s_w, NEG_INF), axis=-1)
        o_w = jnp.einsum('bhgqk,bkhd->bhgqd', p_w, v_win)
        g = lax.dynamic_slice_in_dim(gates, q0, Q_BLOCK, axis=1).reshape(B, Q_BLOCK, N_KV_HEADS, GQA, 3).transpose(0, 2, 3, 1, 4)
        o = g[..., 0:1] * o_c + g[..., 1:2] * o_s + g[..., 2:3] * o_w
        return o.transpose(0, 3, 1, 2, 4).reshape(B, Q_BLOCK, N_HEADS * HEAD_DIM)

    out = lax.map(block_fn, jnp.arange(n_qb))
    return out.transpose(1, 0, 2, 3).reshape(B, S, N_HEADS * HEAD_DIM)


def hybrid_mixer(x, w_in, conv_w, conv_b, rg_w_a, rg_b_a, rg_w_x, rg_b_x, rg_lambda, cmp_k, cmp_v, rel_bias, w_out):
    B, S, _ = x.shape
    proj = x @ w_in
    cuts = [int(c) for c in np.cumsum(PROJ_SIZES)[:-1]]
    xr, gr, q, kc, vc, ksl, vsl, kw, vw, gl = jnp.split(proj, cuts, axis=-1)
    y_rnn = rglru_group(xr, gr, conv_w, conv_b, rg_w_a, rg_b_a, rg_w_x, rg_b_x, rg_lambda)
    kv_shape = (B, S, N_KV_HEADS, HEAD_DIM)
    k_cmp = compress_kv(kc.reshape(kv_shape), *cmp_k)
    v_cmp = compress_kv(vc.reshape(kv_shape), *cmp_v)
    gates = jax.nn.sigmoid(gl.astype(jnp.float32)).reshape(B, S, N_HEADS, 3)
    y_att = nsa_group(q.reshape(B, S, N_HEADS, HEAD_DIM), k_cmp, v_cmp, ksl.reshape(kv_shape), vsl.reshape(kv_shape),
                      kw.reshape(kv_shape), vw.reshape(kv_shape), gates, rel_bias)
    y = jnp.concatenate([y_rnn, y_att], axis=-1).astype(x.dtype)
    return y @ w_out


def moe_ffn(x, router_w, router_b, w_gate, w_up, w_down, sh_gate, sh_up, sh_down):
    f32 = jnp.float32
    B, S, D = x.shape
    xt = x.reshape(-1, D)
    T = xt.shape[0]
    scores = jax.nn.sigmoid((xt @ router_w).astype(f32))
    biased = scores + router_b.astype(f32)
    grp = biased.reshape(T, N_GROUPS, N_EXPERTS // N_GROUPS)
    grp_score = lax.top_k(grp, 2)[0].sum(-1)
    _, top_grp = lax.top_k(grp_score, TOPK_GROUPS)
    grp_mask = jax.nn.one_hot(top_grp, N_GROUPS, dtype=f32).sum(1) > 0
    masked = jnp.where(jnp.repeat(grp_mask, N_EXPERTS // N_GROUPS, axis=1), biased, NEG_INF)
    _, eidx = lax.top_k(masked, TOP_K)
    w = jnp.take_along_axis(scores, eidx, axis=1)
    w = w / jnp.sum(w, axis=-1, keepdims=True) * ROUTED_SCALE
    TK = T * TOP_K
    flat_e = eidx.reshape(-1)
    flat_t = jnp.repeat(jnp.arange(T, dtype=jnp.int32), TOP_K)
    order = jnp.argsort(flat_e)
    se, st, sw = flat_e[order], flat_t[order], w.reshape(-1)[order]
    counts = jnp.bincount(flat_e, length=N_EXPERTS)
    padded = (counts + MOE_BLOCK - 1) // MOE_BLOCK * MOE_BLOCK
    pend = jnp.cumsum(padded)
    pstart = pend - padded
    start = jnp.cumsum(counts) - counts
    dest = pstart[se] + jnp.arange(TK) - start[se]
    n_blocks = -(-TK // MOE_BLOCK) + N_EXPERTS
    R = n_blocks * MOE_BLOCK
    row_tok = jnp.full((R,), T, jnp.int32).at[dest].set(st)
    row_w = jnp.zeros((R,), f32).at[dest].set(sw)
    block_e = jnp.clip(jnp.searchsorted(pend, jnp.arange(n_blocks) * MOE_BLOCK, side='right'), 0, N_EXPERTS - 1)
    x_pad = jnp.concatenate([xt, jnp.zeros((1, D), xt.dtype)], axis=0)

    def body(acc, inp):
        toks, wts, e = inp
        xb = x_pad[toks]
        hb = jax.nn.silu(xb @ w_gate[e]) * (xb @ w_up[e])
        yb = (hb @ w_down[e]).astype(f32) * wts[:, None]
        return acc.at[toks].add(yb), None

    acc, _ = lax.scan(body, jnp.zeros((T + 1, D), f32),
                      (row_tok.reshape(n_blocks, MOE_BLOCK), row_w.reshape(n_blocks, MOE_BLOCK), block_e))
    shared = (jax.nn.silu(xt @ sh_gate) * (xt @ sh_up)) @ sh_down
    return (acc[:T] + shared.astype(f32)).astype(x.dtype).reshape(B, S, D)


def setup_inputs(seed: int = 0) -> dict:
    key = jax.random.key(seed)
    ks = iter(jax.random.split(key, 40))
    f32 = jnp.float32
    L = DEPTH

    def nrm(shape, scale):
        return jax.random.normal(next(ks), shape, f32) * scale

    def gain(shape):
        return 1.0 + 0.02 * jax.random.normal(next(ks), shape, f32)

    u = jax.random.uniform(next(ks), (L, D_RNN), f32, 0.9, 0.999)
    a0 = u ** (1.0 / RG_C)
    return {
        'x': nrm((BATCH, SEQ, D_MODEL), 1.0),
        'ln_in_g': gain((D_MODEL,)),
        'ln_in_b': nrm((D_MODEL,), 0.02),
        'w_in': nrm((L, D_MODEL, N_PROJ), D_MODEL ** -0.5),
        'conv_w': nrm((L, CONV_W, D_RNN), CONV_W ** -0.5),
        'conv_b': nrm((L, D_RNN), 0.02),
        'rg_w_a': nrm((L, RG_BLOCKS, RG_BLOCK_W, RG_BLOCK_W), RG_BLOCK_W ** -0.5),
        'rg_b_a': nrm((L, D_RNN), 0.02),
        'rg_w_x': nrm((L, RG_BLOCKS, RG_BLOCK_W, RG_BLOCK_W), RG_BLOCK_W ** -0.5),
        'rg_b_x': nrm((L, D_RNN), 0.02),
        'rg_lambda': jnp.log(a0) - jnp.log1p(-a0),
        'cmp_pos_k': nrm((L, CMP_LEN, HEAD_DIM), 0.02),
        'cmp_k_w1': nrm((L, CMP_LEN * HEAD_DIM, CMP_HIDDEN), (CMP_LEN * HEAD_DIM) ** -0.5),
        'cmp_k_b1': nrm((L, CMP_HIDDEN), 0.02),
        'cmp_k_w2': nrm((L, CMP_HIDDEN, HEAD_DIM), CMP_HIDDEN ** -0.5),
        'cmp_k_b2': nrm((L, HEAD_DIM), 0.02),
        'cmp_pos_v': nrm((L, CMP_LEN, HEAD_DIM), 0.02),
        'cmp_v_w1': nrm((L, CMP_LEN * HEAD_DIM, CMP_HIDDEN), (CMP_LEN * HEAD_DIM) ** -0.5),
        'cmp_v_b1': nrm((L, CMP_HIDDEN), 0.02),
        'cmp_v_w2': nrm((L, CMP_HIDDEN, HEAD_DIM), CMP_HIDDEN ** -0.5),
        'cmp_v_b2': nrm((L, HEAD_DIM), 0.02),
        'rel_bias': nrm((NUM_BUCKETS, N_HEADS), 0.1),
        'w_out': nrm((L, D_MIX, D_MODEL), BETA * D_MIX ** -0.5),
        'ln1_g': gain((L, D_MODEL)),
        'ln1_b': nrm((L, D_MODEL), 0.02),
        'router_w': nrm((L, D_MODEL, N_EXPERTS), D_MODEL ** -0.5),
        'router_b': nrm((L, N_EXPERTS), 0.01),
        'w_gate': nrm((L, N_EXPERTS, D_MODEL, D_EXPERT), D_MODEL ** -0.5),
        'w_up': nrm((L, N_EXPERTS, D_MODEL, D_EXPERT), D_MODEL ** -0.5),
        'w_down': nrm((L, N_EXPERTS, D_EXPERT, D_MODEL), BETA * D_EXPERT ** -0.5),
        'sh_gate': nrm((L, D_MODEL, D_SHARED), D_MODEL ** -0.5),
        'sh_up': nrm((L, D_MODEL, D_SHARED), D_MODEL ** -0.5),
        'sh_down': nrm((L, D_SHARED, D_MODEL), BETA * D_SHARED ** -0.5),
        'ln2_g': gain((L, D_MODEL)),
        'ln2_b': nrm((L, D_MODEL), 0.02),
    }


def reference(x, ln_in_g, ln_in_b, w_in, conv_w, conv_b, rg_w_a, rg_b_a, rg_w_x, rg_b_x, rg_lambda,
              cmp_pos_k, cmp_k_w1, cmp_k_b1, cmp_k_w2, cmp_k_b2, cmp_pos_v, cmp_v_w1, cmp_v_b1, cmp_v_w2, cmp_v_b2,
              rel_bias, w_out, ln1_g, ln1_b, router_w, router_b, w_gate, w_up, w_down,
              sh_gate, sh_up, sh_down, ln2_g, ln2_b):
    h = layer_norm(x, ln_in_g, ln_in_b)
    for l in range(DEPTH):
        cmp_k = (cmp_pos_k[l], cmp_k_w1[l], cmp_k_b1[l], cmp_k_w2[l], cmp_k_b2[l])
        cmp_v = (cmp_pos_v[l], cmp_v_w1[l], cmp_v_b1[l], cmp_v_w2[l], cmp_v_b2[l])
        mix = hybrid_mixer(h, w_in[l], conv_w[l], conv_b[l], rg_w_a[l], rg_b_a[l], rg_w_x[l], rg_b_x[l], rg_lambda[l],
                           cmp_k, cmp_v, rel_bias, w_out[l])
        h = layer_norm(ALPHA * h + mix, ln1_g[l], ln1_b[l])
        ffn = moe_ffn(h, router_w[l], router_b[l], w_gate[l], w_up[l], w_down[l], sh_gate[l], sh_up[l], sh_down[l])
        h = layer_norm(ALPHA * h + ffn, ln2_g[l], ln2_b[l])
    return h
```

```python
import functools
import math

import numpy as np
import jax
import jax.numpy as jnp
from jax import lax
from jax.experimental import pallas as pl
from jax.experimental.pallas import tpu as pltpu

F32 = jnp.float32
BF16 = jnp.bfloat16

DEPTH = 1
D_RNN_FRAC = 2
RG_BLOCKS = 8
CONV_W = 4
RG_C = 8.0
N_HEADS = 8
N_KV_HEADS = 2
GQA = N_HEADS // N_KV_HEADS
CMP_STRIDE = 16
CMP_LEN = 2 * CMP_STRIDE
SEL_BLOCK = 64
SEL_TOPN = 16
WINDOW = 512
Q_BLOCK = 128
NUM_BUCKETS = 32
MAX_DISTANCE = 128
N_EXPERTS = 256
TOP_K = 8
N_GROUPS = 8
TOPK_GROUPS = 4
ROUTED_SCALE = 2.5
ALPHA = (2 * DEPTH) ** 0.25
LN_EPS = 1e-5
NEG_INF = -1e30
FORCE_SCORE = 1e9

LANES = 128
SUBLANES = 8
VMEM_LIMIT_BYTES = 56 * 1024 * 1024

PROJ_ROWS = 512
RG_ROWS = 256
ROUTE_COLS = 512
DISPATCH_ROWS = 256
EXPERT_ROWS = 256
COMBINE_ROWS = 128
SEL_FAR_KEYS = 512
CMP_NEAR = 32
CMP_PAD = CMP_NEAR - Q_BLOCK // CMP_STRIDE
SEL_NEAR = 2 * Q_BLOCK
KA_WIDTH = 256


def _cparams(*sem):
    return pltpu.CompilerParams(dimension_semantics=sem, vmem_limit_bytes=VMEM_LIMIT_BYTES)


def _sigmoid(x):
    return 1.0 / (1.0 + jnp.exp(-x))


def _gelu_tanh(x):
    return 0.5 * x * (1.0 + jnp.tanh(math.sqrt(2.0 / math.pi) * (x + 0.044715 * (x * x * x))))


def _layer_norm(x, g, b):
    mu = jnp.mean(x, axis=-1, keepdims=True)
    xc = x - mu
    var = jnp.mean(xc * xc, axis=-1, keepdims=True)
    return xc * lax.rsqrt(var + LN_EPS) * g + b


def _dot(a, b):
    return jnp.dot(a, b, preferred_element_type=F32)


def _dot_nt(a, b):
    return lax.dot_general(a, b, (((1,), (1,)), ((), ())), preferred_element_type=F32)


def _ln_inproj_kernel(x_ref, g_ref, b_ref, wrg_ref, watt_ref, wgl_ref,
                      h_ref, rg_ref, att_ref, gate_ref, *, q_cols, q_scale):
    h = _layer_norm(x_ref[...], g_ref[...], b_ref[...])
    h_ref[...] = h
    hb = h.astype(BF16)
    rg_ref[...] = _dot(hb, wrg_ref[...])
    att = _dot(hb, watt_ref[...])
    att_ref[:, :q_cols] = (att[:, :q_cols] * q_scale).astype(BF16)
    att_ref[:, q_cols:] = att[:, q_cols:].astype(BF16)
    gate_ref[...] = _sigmoid(_dot(hb, wgl_ref[...]))


def _ln_inproj(x2, g, b, w_rg, w_att, w_gl, *, q_cols, q_scale):
    T, D = x2.shape
    n_rg, n_att, n_gl = w_rg.shape[1], w_att.shape[1], w_gl.shape[1]
    tm = PROJ_ROWS
    row = lambda i: (i, 0)
    full = lambda i: (0, 0)
    return pl.pallas_call(
        functools.partial(_ln_inproj_kernel, q_cols=q_cols, q_scale=q_scale),
        grid=(T // tm,),
        in_specs=[pl.BlockSpec((tm, D), row), pl.BlockSpec((1, D), full), pl.BlockSpec((1, D), full),
                  pl.BlockSpec((D, n_rg), full), pl.BlockSpec((D, n_att), full), pl.BlockSpec((D, n_gl), full)],
        out_specs=[pl.BlockSpec((tm, D), row), pl.BlockSpec((tm, n_rg), row),
                   pl.BlockSpec((tm, n_att), row), pl.BlockSpec((tm, n_gl), row)],
        out_shape=[jax.ShapeDtypeStruct((T, D), F32), jax.ShapeDtypeStruct((T, n_rg), F32),
                   jax.ShapeDtypeStruct((T, n_att), BF16), jax.ShapeDtypeStruct((T, n_gl), F32)],
        compiler_params=_cparams("arbitrary"),
        name="ln_inproj",
    )(x2, g, b, w_rg, w_att, w_gl)


def _rglru_kernel(rg_ref, cw_ref, cb_ref, wa_ref, ba_ref, wx_ref, bx_ref, lam_ref, y_ref,
                  xprev_scr, hprev_scr, *, d_rnn, rows):
    s = pl.program_id(1)

    @pl.when(s == 0)
    def _():
        xprev_scr[...] = jnp.zeros_like(xprev_scr)
        hprev_scr[...] = jnp.zeros_like(hprev_scr)

    xr = rg_ref[0, :, :d_rnn]
    gr = rg_ref[0, :, d_rnn:]
    xcat = jnp.concatenate([xprev_scr[...], xr], axis=0)
    xc = cw_ref[CONV_W - 1:CONV_W, :] * xr + cb_ref[...]
    for k in range(1, CONV_W):
        shifted = pltpu.roll(xcat, k, 0)[SUBLANES:, :]
        xc = xc + cw_ref[CONV_W - 1 - k:CONV_W - k, :] * shifted
    xprev_scr[...] = xr[rows - SUBLANES:, :]

    xcb = xc.astype(BF16)
    r = _sigmoid(_dot(xcb, wa_ref[...]) + ba_ref[...])
    i = _sigmoid(_dot(xcb, wx_ref[...]) + bx_ref[...])
    neg_lam = -lam_ref[...]
    softplus = jnp.maximum(neg_lam, 0.0) + jnp.log1p(jnp.exp(-jnp.abs(neg_lam)))
    log_a = -RG_C * r * softplus
    a = jnp.exp(log_a)
    u = jnp.sqrt(jnp.tanh(-log_a) * (1.0 + a * a)) * (i * xc)

    t_idx = lax.broadcasted_iota(jnp.int32, (rows, d_rnn), 0)
    d = 1
    while d < rows:
        keep = t_idx >= d
        a_sh = jnp.where(keep, pltpu.roll(a, d, 0), 1.0)
        u_sh = jnp.where(keep, pltpu.roll(u, d, 0), 0.0)
        u = u + a * u_sh
        a = a * a_sh
        d *= 2
    h = u + a * hprev_scr[0:1, :]
    hprev_scr[...] = jnp.broadcast_to(h[rows - 1:rows, :], hprev_scr.shape)
    y_ref[0] = (h * _gelu_tanh(gr)).astype(y_ref.dtype)


def _rglru(rg3, conv_w, conv_b, wa_bd, b_a, wx_bd, b_x, lam):
    B, S, two_d = rg3.shape
    d = two_d // 2
    rows = RG_ROWS
    full = lambda b, s: (0, 0)
    return pl.pallas_call(
        functools.partial(_rglru_kernel, d_rnn=d, rows=rows),
        grid=(B, S // rows),
        in_specs=[pl.BlockSpec((1, rows, two_d), lambda b, s: (b, s, 0)),
                  pl.BlockSpec((CONV_W, d), full), pl.BlockSpec((1, d), full),
                  pl.BlockSpec((d, d), full), pl.BlockSpec((1, d), full),
                  pl.BlockSpec((d, d), full), pl.BlockSpec((1, d), full), pl.BlockSpec((1, d), full)],
        out_specs=pl.BlockSpec((1, rows, d), lambda b, s: (b, s, 0)),
        out_shape=jax.ShapeDtypeStruct((B, S, d), BF16),
        scratch_shapes=[pltpu.VMEM((SUBLANES, d), F32), pltpu.VMEM((SUBLANES, d), F32)],
        compiler_params=_cparams("arbitrary", "arbitrary"),
        name="rglru",
    )(rg3, conv_w, conv_b, wa_bd, b_a, wx_bd, b_x, lam)


def _compress_kernel(x_ref, pos_ref, w1a_ref, w1b_ref, b1_ref, w2_ref, b2_ref, o_ref, *, n_chunks):
    x = x_ref[0, 0, 0]
    w1a = w1a_ref[0]
    w1b = w1b_ref[0]
    pos = pos_ref[0]
    half = x.shape[1]
    pos_term = _dot(pos[:, :half], w1a) + _dot(pos[:, half:], w1b)
    first = _dot(x, w1a)
    second = _dot(x, w1b)
    hidden = first + pltpu.roll(second, n_chunks - 1, 0) + pos_term[0:1, :] + b1_ref[0]
    out = _dot(_gelu_tanh(hidden).astype(BF16), w2_ref[0]) + b2_ref[0]
    rows = lax.broadcasted_iota(jnp.int32, out.shape, 0)
    o_ref[0, 0, 0] = jnp.where(rows < n_chunks - 1, out, 0.0).astype(o_ref.dtype)


def _compress(chunks, pos, w1a, w1b, b1, w2, b2):
    two, B, HKV, NCH, half = chunks.shape
    hid = w1a.shape[-1]
    dh = w2.shape[-1]
    kv = lambda c, b, h: (c, 0, 0)
    return pl.pallas_call(
        functools.partial(_compress_kernel, n_chunks=NCH),
        grid=(two, B, HKV),
        in_specs=[pl.BlockSpec((1, 1, 1, NCH, half), lambda c, b, h: (c, b, h, 0, 0)),
                  pl.BlockSpec((1, SUBLANES, 2 * half), kv),
                  pl.BlockSpec((1, half, hid), kv), pl.BlockSpec((1, half, hid), kv),
                  pl.BlockSpec((1, 1, hid), kv), pl.BlockSpec((1, hid, dh), kv), pl.BlockSpec((1, 1, dh), kv)],
        out_specs=pl.BlockSpec((1, 1, 1, NCH, dh), lambda c, b, h: (c, b, h, 0, 0)),
        out_shape=jax.ShapeDtypeStruct((two, B, HKV, NCH, dh), BF16),
        compiler_params=_cparams("arbitrary", "arbitrary", "arbitrary"),
        name="compress_kv",
    )(chunks, pos, w1a, w1b, b1, w2, b2)


def _t5_bucket_np(dist):
    n = np.maximum(dist, 0)
    max_exact = NUM_BUCKETS // 2
    nf = np.maximum(n, 1).astype(np.float32)
    large = max_exact + (np.log(nf / np.float32(max_exact)) / np.float32(math.log(MAX_DISTANCE / max_exact))
                         * np.float32(NUM_BUCKETS - max_exact)).astype(np.int32)
    return np.where(n < max_exact, n, np.minimum(large, NUM_BUCKETS - 1)).astype(np.int32)


def _bias_table(rel_bias, dist, valid):
    q, k = dist.shape
    tab = rel_bias.astype(F32)[jnp.asarray(_t5_bucket_np(dist))]
    tab = jnp.where(jnp.asarray(valid)[:, :, None], tab, NEG_INF)
    return tab.transpose(2, 0, 1).reshape(N_KV_HEADS, GQA * q, k)


def _nsa_kernel(q_ref, kcp_ref, vcp_ref, ov_ref, ka_ref, vs_ref, kw_ref, vw_ref, gate_ref,
                b31_ref, bcn_ref, bw_ref, bsn_ref, o_ref, qa_scr, m_scr, l_scr, acc_scr, *, dh):
    qb = pl.program_id(2)
    q0 = qb * Q_BLOCK
    rows = GQA * Q_BLOCK
    q4 = q_ref[0, 0, 0]
    b31 = b31_ref[0]
    row_q = lax.broadcasted_iota(jnp.int32, (rows, 1), 0) % Q_BLOCK

    n_cp = kcp_ref.shape[2]
    cpb = Q_BLOCK // CMP_STRIDE
    s_far = _dot_nt(q4, kcp_ref[0, 0]) + b31
    col = lax.broadcasted_iota(jnp.int32, (rows, n_cp), 1)
    s_far = jnp.where((col >= CMP_PAD) & (col < cpb * qb), s_far, NEG_INF)
    near0 = pl.multiple_of(cpb * qb, SUBLANES)
    s_near = _dot_nt(q4, kcp_ref[0, 0, pl.ds(near0, CMP_NEAR), :]) + bcn_ref[0]
    coln = lax.broadcasted_iota(jnp.int32, (rows, CMP_NEAR), 1)
    s_near = jnp.where(coln + cpb * qb >= CMP_PAD, s_near, NEG_INF)
    m_c = jnp.maximum(jnp.max(s_far, axis=1, keepdims=True), jnp.max(s_near, axis=1, keepdims=True))
    p_far = jnp.exp(s_far - m_c)
    p_near = jnp.exp(s_near - m_c)
    l_c = jnp.sum(p_far, axis=1, keepdims=True) + jnp.sum(p_near, axis=1, keepdims=True)
    any_valid = (q0 + row_q >= CMP_LEN - 1).astype(F32)
    scale_c = any_valid / l_c
    pf_hi = p_far.astype(BF16)
    pn_hi = p_near.astype(BF16)
    vc_near = vcp_ref[0, 0, pl.ds(near0, CMP_NEAR), :]
    o_c = (_dot(pf_hi, vcp_ref[0, 0]) + _dot(pn_hi, vc_near)) * scale_c
    pf_lo = (p_far - pf_hi.astype(F32)).astype(BF16)
    pn_lo = (p_near - pn_hi.astype(F32)).astype(BF16)
    ov_near = ov_ref[pl.ds(near0, CMP_NEAR), :]
    imp4 = (_dot(pf_hi, ov_ref[...]) + _dot(pf_lo, ov_ref[...])
            + _dot(pn_hi, ov_near) + _dot(pn_lo, ov_near)) * scale_c
    imp = imp4[0:Q_BLOCK]
    for g in range(1, GQA):
        imp = imp + imp4[g * Q_BLOCK:(g + 1) * Q_BLOCK]

    n_sb = imp.shape[1]
    blk = lax.broadcasted_iota(jnp.int32, (Q_BLOCK, n_sb), 1)
    qq = lax.broadcasted_iota(jnp.int32, (Q_BLOCK, n_sb), 0)
    cur = (q0 + qq) // SEL_BLOCK
    forced = (blk == 0) | (blk == cur) | (blk == cur - 1)
    score = jnp.where(forced, FORCE_SCORE, jnp.where(blk <= cur, imp, -1.0))
    v_t = score.T
    r_t = lax.broadcasted_iota(jnp.int32, (n_sb, Q_BLOCK), 0)
    sel_t = jnp.zeros((n_sb, Q_BLOCK), F32)
    for _ in range(min(SEL_TOPN, n_sb)):
        mx = jnp.max(v_t, axis=0, keepdims=True)
        idx = jnp.min(jnp.where(v_t == mx, r_t, n_sb), axis=0, keepdims=True)
        hit = r_t == idx
        sel_t = jnp.where(hit, 1.0, sel_t)
        v_t = jnp.where(hit, -jnp.inf, v_t)
    sel = sel_t.T
    near_blk0 = 2 * qb - SEL_NEAR // SEL_BLOCK // 2
    neg_near = jnp.where(sel > 0.5, 0.0, NEG_INF)
    neg_far = jnp.where(blk < near_blk0, neg_near, NEG_INF)

    qa_scr[:, n_sb:n_sb + dh] = q4
    qa_scr[:, n_sb + dh:] = jnp.zeros((rows, KA_WIDTH - n_sb - dh), BF16)
    for g in range(GQA):
        qa_scr[g * Q_BLOCK:(g + 1) * Q_BLOCK, :n_sb] = neg_near.astype(BF16)
    ks0 = pl.multiple_of(q0, Q_BLOCK)
    s_n = _dot_nt(qa_scr[...], ka_ref[0, 0, pl.ds(ks0, SEL_NEAR), :]) + bsn_ref[0]
    colk = lax.broadcasted_iota(jnp.int32, (rows, SEL_NEAR), 1)
    s_n = jnp.where(colk + q0 >= SEL_NEAR // 2, s_n, NEG_INF)
    m0 = jnp.max(s_n, axis=1, keepdims=True)
    p_n = jnp.exp(s_n - m0)
    m_scr[...] = m0
    l_scr[...] = jnp.sum(p_n, axis=1, keepdims=True)
    acc_scr[...] = _dot(p_n.astype(BF16), vs_ref[0, 0, pl.ds(ks0, SEL_NEAR), :])

    for g in range(GQA):
        qa_scr[g * Q_BLOCK:(g + 1) * Q_BLOCK, :n_sb] = neg_far.astype(BF16)
    n_far = jnp.maximum(q0 - SEL_NEAR // 2 + SEL_FAR_KEYS - 1, 0) // SEL_FAR_KEYS

    def far_body(t, carry):
        k0 = pl.multiple_of(SEL_NEAR // 2 + t * SEL_FAR_KEYS, SEL_NEAR // 2)
        s_f = _dot_nt(qa_scr[...], ka_ref[0, 0, pl.ds(k0, SEL_FAR_KEYS), :]) + b31
        m_old = m_scr[...]
        m_new = jnp.maximum(m_old, jnp.max(s_f, axis=1, keepdims=True))
        alpha = jnp.exp(m_old - m_new)
        p_f = jnp.exp(s_f - m_new)
        l_scr[...] = alpha * l_scr[...] + jnp.sum(p_f, axis=1, keepdims=True)
        acc_scr[...] = alpha * acc_scr[...] + _dot(p_f.astype(BF16), vs_ref[0, 0, pl.ds(k0, SEL_FAR_KEYS), :])
        m_scr[...] = m_new
        return carry

    lax.fori_loop(0, n_far, far_body, 0)
    o_s = acc_scr[...] / l_scr[...]

    n_win = WINDOW + Q_BLOCK
    s_w = _dot_nt(q4, kw_ref[0, 0, pl.ds(ks0, n_win), :]) + bw_ref[0]
    colw = lax.broadcasted_iota(jnp.int32, (rows, n_win), 1)
    s_w = jnp.where(colw + q0 >= WINDOW, s_w, NEG_INF)
    m_w = jnp.max(s_w, axis=1, keepdims=True)
    p_w = jnp.exp(s_w - m_w)
    l_w = jnp.sum(p_w, axis=1, keepdims=True)
    o_w = _dot(p_w.astype(BF16), vw_ref[0, 0, pl.ds(ks0, n_win), :]) / l_w

    gates = gate_ref[0, 0]
    for g in range(GQA):
        sl = slice(g * Q_BLOCK, (g + 1) * Q_BLOCK)
        o = (gates[:, 3 * g:3 * g + 1] * o_c[sl] + gates[:, 3 * g + 1:3 * g + 2] * o_s[sl]
             + gates[:, 3 * g + 2:3 * g + 3] * o_w[sl])
        o_ref[0, 0, 0, sl, :] = o.astype(o_ref.dtype)


def _nsa(q_r, kc_pad, vc_pad, ov_pad, ka_pad, vs_pad, kw_pad, vw_pad, gates_r, b31, bcn, bw, bsn):
    B, HKV, NQB, rows, dh = q_r.shape
    n_sb = ov_pad.shape[1]
    per_bh = lambda b, h, i: (b, h, 0, 0)
    per_h = lambda b, h, i: (h, 0, 0)
    return pl.pallas_call(
        functools.partial(_nsa_kernel, dh=dh),
        grid=(B, HKV, NQB),
        in_specs=[pl.BlockSpec((1, 1, 1, rows, dh), lambda b, h, i: (b, h, i, 0, 0)),
                  pl.BlockSpec((1, 1) + kc_pad.shape[2:], per_bh),
                  pl.BlockSpec((1, 1) + vc_pad.shape[2:], per_bh),
                  pl.BlockSpec(ov_pad.shape, lambda b, h, i: (0, 0)),
                  pl.BlockSpec((1, 1) + ka_pad.shape[2:], per_bh),
                  pl.BlockSpec((1, 1) + vs_pad.shape[2:], per_bh),
                  pl.BlockSpec((1, 1) + kw_pad.shape[2:], per_bh),
                  pl.BlockSpec((1, 1) + vw_pad.shape[2:], per_bh),
                  pl.BlockSpec((1, 1, Q_BLOCK, LANES), lambda b, h, i: (b, h, i, 0)),
                  pl.BlockSpec((1,) + b31.shape[1:], per_h),
                  pl.BlockSpec((1,) + bcn.shape[1:], per_h),
                  pl.BlockSpec((1,) + bw.shape[1:], per_h),
                  pl.BlockSpec((1,) + bsn.shape[1:], per_h)],
        out_specs=pl.BlockSpec((1, 1, 1, rows, dh), lambda b, h, i: (b, h, i, 0, 0)),
        out_shape=jax.ShapeDtypeStruct((B, HKV, NQB, rows, dh), BF16),
        scratch_shapes=[pltpu.VMEM((rows, KA_WIDTH), BF16), pltpu.VMEM((rows, 1), F32),
                        pltpu.VMEM((rows, 1), F32), pltpu.VMEM((rows, dh), F32)],
        compiler_params=_cparams("arbitrary", "arbitrary", "arbitrary"),
        name="nsa_attention",
    )(q_r, kc_pad, vc_pad, ov_pad, ka_pad, vs_pad, kw_pad, vw_pad, gates_r, b31, bcn, bw, bsn)


def _attention(att, gates, k_cmp, v_cmp, rel_bias, B, S, dh):
    HKV = N_KV_HEADS
    H = N_HEADS
    nqb = S // Q_BLOCK
    n_sb = S // SEL_BLOCK
    assert n_sb == LANES, "selection blocks are laid out on one lane tile"
    kvw = HKV * dh
    c0 = H * dh

    def kv_heads(j):
        a = att[:, c0 + j * kvw:c0 + (j + 1) * kvw].reshape(B, S, HKV, dh)
        return a.transpose(0, 2, 1, 3)

    q_r = (att[:, :c0].reshape(B, nqb, Q_BLOCK, HKV, GQA, dh).transpose(0, 3, 1, 4, 2, 5)
           .reshape(B, HKV, nqb, GQA * Q_BLOCK, dh))
    ks, vs, kw, vw = kv_heads(2), kv_heads(3), kv_heads(4), kv_heads(5)
    front = lambda a, n: jnp.pad(a, ((0, 0), (0, 0), (n, 0), (0, 0)))
    onehot = np.zeros((S, KA_WIDTH), np.float32)
    onehot[np.arange(S), np.arange(S) // SEL_BLOCK] = 1.0
    ka = jnp.concatenate([jnp.broadcast_to(jnp.asarray(onehot[:, :n_sb], BF16), (B, HKV, S, n_sb)), ks,
                          jnp.zeros((B, HKV, S, KA_WIDTH - n_sb - dh), BF16)], axis=-1)
    ka_pad = front(ka, SEL_NEAR // 2)
    vs_pad = front(vs, SEL_NEAR // 2)
    kw_pad = front(kw, WINDOW)
    vw_pad = front(vw, WINDOW)
    kc_pad = front(k_cmp, CMP_PAD)
    vc_pad = front(v_cmp, CMP_PAD)
    n_c = S // CMP_STRIDE
    c_start = np.arange(n_c)[:, None] * CMP_STRIDE
    s_start = np.arange(n_sb)[None, :] * SEL_BLOCK
    overlap = ((c_start < s_start + SEL_BLOCK) & (c_start + CMP_LEN > s_start)).astype(np.float32)
    overlap[n_c - 1] = 0.0
    ov_pad = jnp.asarray(np.concatenate([np.zeros((CMP_PAD, n_sb), np.float32), overlap]), BF16)

    gates_r = gates[:, :3 * H].reshape(B, S, HKV, 3 * GQA).transpose(0, 2, 1, 3)
    gates_r = jnp.pad(gates_r, ((0, 0), (0, 0), (0, 0), (0, LANES - 3 * GQA)))

    qi = np.arange(Q_BLOCK)[:, None]
    d_cn = qi - CMP_STRIDE * (np.arange(CMP_NEAR)[None, :] - CMP_PAD) - (CMP_LEN - 1)
    d_w = qi + WINDOW - np.arange(WINDOW + Q_BLOCK)[None, :]
    d_sn = qi + SEL_NEAR // 2 - np.arange(SEL_NEAR)[None, :]
    bcn = _bias_table(rel_bias, d_cn, d_cn >= 0)
    bw = _bias_table(rel_bias, d_w, (d_w >= 0) & (d_w < WINDOW))
    bsn = _bias_table(rel_bias, d_sn, d_sn >= 0)
    b31 = jnp.repeat(rel_bias.astype(F32)[NUM_BUCKETS - 1].reshape(HKV, GQA), Q_BLOCK, axis=1)[:, :, None]

    o = _nsa(q_r, kc_pad, vc_pad, ov_pad, ka_pad, vs_pad, kw_pad, vw_pad, gates_r, b31, bcn, bw, bsn)
    o = o.reshape(B, HKV, nqb, GQA, Q_BLOCK, dh).transpose(0, 2, 4, 1, 3, 5)
    return o.reshape(B * S, H * dh)


def _outproj_kernel(h_ref, yr_ref, ya_ref, wr_ref, wa_ref, g_ref, b_ref, o_ref):
    mix = _dot(yr_ref[...], wr_ref[...]) + _dot(ya_ref[...], wa_ref[...])
    o_ref[...] = _layer_norm(ALPHA * h_ref[...] + mix, g_ref[...], b_ref[...])


def _outproj_ln(h, y_rnn, y_att, w_r, w_a, g, b):
    T, D = h.shape
    dr, da = y_rnn.shape[1], y_att.shape[1]
    tm = PROJ_ROWS
    row = lambda i: (i, 0)
    full = lambda i: (0, 0)
    return pl.pallas_call(
        _outproj_kernel,
        grid=(T // tm,),
        in_specs=[pl.BlockSpec((tm, D), row), pl.BlockSpec((tm, dr), row), pl.BlockSpec((tm, da), row),
                  pl.BlockSpec((dr, D), full), pl.BlockSpec((da, D), full),
                  pl.BlockSpec((1, D), full), pl.BlockSpec((1, D), full)],
        out_specs=pl.BlockSpec((tm, D), row),
        out_shape=jax.ShapeDtypeStruct((T, D), F32),
        compiler_params=_cparams("arbitrary"),
        name="outproj_ln",
    )(h, y_rnn, y_att, w_r, w_a, g, b)


def _route_kernel(h_ref, wh_ref, wl_ref, rb_ref, tri_ref, e_ref, p_ref, w_ref, cnt_ref, carry_scr, *, cols):
    i = pl.program_id(0)

    @pl.when(i == 0)
    def _():
        carry_scr[...] = jnp.zeros_like(carry_scr)

    h = h_ref[...]
    h_hi = h.astype(BF16)
    h_lo = (h - h_hi.astype(F32)).astype(BF16)
    logits = _dot_nt(wh_ref[...], h_hi) + (_dot_nt(wl_ref[...], h_hi) + _dot_nt(wh_ref[...], h_lo))
    scores = _sigmoid(logits)
    biased = scores + rb_ref[...]
    gsz = N_EXPERTS // N_GROUPS
    gscore = []
    for g in range(N_GROUPS):
        blk = biased[g * gsz:(g + 1) * gsz]
        m1 = jnp.max(blk, axis=0, keepdims=True)
        is_m1 = blk == m1
        n_m1 = jnp.sum(jnp.where(is_m1, 1.0, 0.0), axis=0, keepdims=True)
        m2 = jnp.max(jnp.where(is_m1, -jnp.inf, blk), axis=0, keepdims=True)
        gscore.append(m1 + jnp.where(n_m1 > 1.5, m1, m2))
    masked = []
    for g in range(N_GROUPS):
        rank = jnp.zeros_like(gscore[g])
        for o in range(N_GROUPS):
            if o == g:
                continue
            ahead = (gscore[o] > gscore[g]) | (gscore[o] == gscore[g]) if o < g else gscore[o] > gscore[g]
            rank = rank + jnp.where(ahead, 1.0, 0.0)
        keep = rank < TOPK_GROUPS - 0.5
        masked.append(jnp.where(keep, biased[g * gsz:(g + 1) * gsz], NEG_INF))
    v = jnp.concatenate(masked, axis=0)
    rows = lax.broadcasted_iota(jnp.int32, (N_EXPERTS, cols), 0)
    sel = jnp.zeros((N_EXPERTS, cols), F32)
    idxs, svals = [], []
    for _ in range(TOP_K):
        mx = jnp.max(v, axis=0, keepdims=True)
        idx = jnp.min(jnp.where(v == mx, rows, N_EXPERTS), axis=0, keepdims=True)
        hit = rows == idx
        idxs.append(idx)
        svals.append(jnp.sum(jnp.where(hit, scores, 0.0), axis=0, keepdims=True))
        sel = jnp.where(hit, 1.0, sel)
        v = jnp.where(hit, -jnp.inf, v)
    total = svals[0]
    for k in range(1, TOP_K):
        total = total + svals[k]
    prefix = _dot(sel.astype(BF16), tri_ref[...]) + carry_scr[:, 0:1]
    for k in range(TOP_K):
        hit = rows == idxs[k]
        e_ref[k:k + 1, :] = idxs[k]
        p_ref[k:k + 1, :] = jnp.sum(jnp.where(hit, prefix, 0.0), axis=0, keepdims=True).astype(jnp.int32)
        w_ref[k:k + 1, :] = svals[k] / total * ROUTED_SCALE
    new_carry = carry_scr[...] + jnp.sum(sel, axis=1, keepdims=True)
    carry_scr[...] = new_carry
    cnt_ref[...] = new_carry


def _route(h1, rw_hi, rw_lo, rb, tri):
    T, D = h1.shape
    cols = ROUTE_COLS
    full = lambda i: (0, 0)
    out_col = pl.BlockSpec((TOP_K, cols), lambda i: (0, i))
    return pl.pallas_call(
        functools.partial(_route_kernel, cols=cols),
        grid=(T // cols,),
        in_specs=[pl.BlockSpec((cols, D), lambda i: (i, 0)),
                  pl.BlockSpec((N_EXPERTS, D), full), pl.BlockSpec((N_EXPERTS, D), full),
                  pl.BlockSpec((N_EXPERTS, 1), full), pl.BlockSpec((cols, cols), full)],
        out_specs=[out_col, out_col, out_col, pl.BlockSpec((N_EXPERTS, LANES), full)],
        out_shape=[jax.ShapeDtypeStruct((TOP_K, T), jnp.int32), jax.ShapeDtypeStruct((TOP_K, T), jnp.int32),
                   jax.ShapeDtypeStruct((TOP_K, T), F32), jax.ShapeDtypeStruct((N_EXPERTS, LANES), F32)],
        scratch_shapes=[pltpu.VMEM((N_EXPERTS, LANES), F32)],
        compiler_params=_cparams("arbitrary"),
        name="router",
    )(h1, rw_hi, rw_lo, rb, tri)


def _tile_major(a, tile):
    k, t = a.shape
    return a.reshape(k, t // tile, tile).transpose(1, 0, 2).reshape(-1)


def _dispatch_kernel(pstart_ref, h_ref, e_hbm, p_hbm, xs_in, xs_out, e_smem, p_smem, idx_sem, row_sem, *, rows):
    del xs_in
    i = pl.program_id(0)
    n = TOP_K * rows
    base = pl.multiple_of(i * n, n)
    ce = pltpu.make_async_copy(e_hbm.at[pl.ds(base, n)], e_smem, idx_sem.at[0])
    cp = pltpu.make_async_copy(p_hbm.at[pl.ds(base, n)], p_smem, idx_sem.at[1])
    ce.start()
    cp.start()
    ce.wait()
    cp.wait()

    def row_copy(j):
        t = j & (rows - 1)
        dst = pstart_ref[e_smem[j]] + p_smem[j]
        return pltpu.make_async_copy(h_ref.at[pl.ds(t, 1), :], xs_out.at[pl.ds(dst, 1), :], row_sem)

    def issue(j, c):
        row_copy(j).start()
        return c

    def drain(j, c):
        row_copy(j).wait()
        return c

    lax.fori_loop(0, n, issue, 0)
    lax.fori_loop(0, n, drain, 0)


def _dispatch(h1, e_flat, p_flat, pstart, xs_init):
    T, D = h1.shape
    rows = DISPATCH_ROWS
    n = TOP_K * rows
    return pl.pallas_call(
        functools.partial(_dispatch_kernel, rows=rows),
        grid_spec=pltpu.PrefetchScalarGridSpec(
            num_scalar_prefetch=1,
            grid=(T // rows,),
            in_specs=[pl.BlockSpec((rows, D), lambda i, ps: (i, 0)),
                      pl.BlockSpec(memory_space=pl.ANY), pl.BlockSpec(memory_space=pl.ANY),
                      pl.BlockSpec(memory_space=pl.ANY)],
            out_specs=pl.BlockSpec(memory_space=pl.ANY),
            scratch_shapes=[pltpu.SMEM((n,), jnp.int32), pltpu.SMEM((n,), jnp.int32),
                            pltpu.SemaphoreType.DMA((2,)), pltpu.SemaphoreType.DMA(())]),
        out_shape=jax.ShapeDtypeStruct(xs_init.shape, xs_init.dtype),
        input_output_aliases={4: 0},
        compiler_params=_cparams("arbitrary"),
        name="moe_dispatch",
    )(pstart, h1, e_flat, p_flat, xs_init)


def _expert_kernel(be_ref, na_ref, x_ref, wg_ref, wu_ref, wd_ref, y_ref):
    del be_ref
    i = pl.program_id(0)

    @pl.when(i < na_ref[0])
    def _():
        x = x_ref[...].astype(BF16)
        g = _dot(x, wg_ref[0].astype(BF16))
        u = _dot(x, wu_ref[0].astype(BF16))
        hmid = (g * _sigmoid(g)) * u
        y_ref[...] = _dot(hmid.astype(BF16), wd_ref[0].astype(BF16))

    @pl.when(i >= na_ref[0])
    def _():
        y_ref[...] = jnp.zeros_like(y_ref)


def _experts(xs, w_gate, w_up, w_down, block_e, n_active):
    R, D = xs.shape
    E, _, de = w_gate.shape
    bm = EXPERT_ROWS
    x_map = lambda i, be, na: (jnp.minimum(i, jnp.maximum(na[0] - 1, 0)), 0)
    w_map = lambda i, be, na: (be[i], 0, 0)
    return pl.pallas_call(
        _expert_kernel,
        grid_spec=pltpu.PrefetchScalarGridSpec(
            num_scalar_prefetch=2,
            grid=(R // bm,),
            in_specs=[pl.BlockSpec((bm, D), x_map), pl.BlockSpec((1, D, de), w_map),
                      pl.BlockSpec((1, D, de), w_map), pl.BlockSpec((1, de, D), w_map)],
            out_specs=pl.BlockSpec((bm, D), lambda i, be, na: (i, 0))),
        out_shape=jax.ShapeDtypeStruct((R, D), F32),
        compiler_params=_cparams("arbitrary"),
        name="moe_experts",
    )(block_e, n_active, xs, w_gate, w_up, w_down)


def _combine_kernel(pstart_ref, h_ref, wk_ref, e_hbm, p_hbm, ys_hbm, sg_ref, su_ref, sd_ref, g_ref, b_ref,
                    o_ref, e_smem, p_smem, buf, idx_sem, row_sem, *, rows):
    i = pl.program_id(0)
    n = TOP_K * rows
    base = pl.multiple_of(i * n, n)
    ce = pltpu.make_async_copy(e_hbm.at[pl.ds(base, n)], e_smem, idx_sem.at[0])
    cp = pltpu.make_async_copy(p_hbm.at[pl.ds(base, n)], p_smem, idx_sem.at[1])
    ce.start()
    cp.start()
    ce.wait()
    cp.wait()

    def row_copy(j):
        src = pstart_ref[e_smem[j]] + p_smem[j]
        return pltpu.make_async_copy(ys_hbm.at[pl.ds(src, 1), :], buf.at[pl.ds(j, 1), :], row_sem)

    def issue(j, c):
        row_copy(j).start()
        return c

    def drain(j, c):
        row_copy(j).wait()
        return c

    lax.fori_loop(0, n, issue, 0)
    h = h_ref[...]
    hb = h.astype(BF16)
    g = _dot(hb, sg_ref[...])
    u = _dot(hb, su_ref[...])
    shared = _dot(((g * _sigmoid(g)) * u).astype(BF16), sd_ref[...])
    lax.fori_loop(0, n, drain, 0)
    wk = wk_ref[...]
    acc = wk[:, 0:1] * buf[0:rows, :]
    for k in range(1, TOP_K):
        acc = acc + wk[:, k:k + 1] * buf[k * rows:(k + 1) * rows, :]
    o_ref[...] = _layer_norm(ALPHA * h + (acc + shared), g_ref[...], b_ref[...])


def _combine(h1, wk_t, e_flat, p_flat, pstart, ys, sh_gate, sh_up, sh_down, g, b):
    T, D = h1.shape
    ds = sh_gate.shape[1]
    rows = COMBINE_ROWS
    n = TOP_K * rows
    full = lambda i, ps: (0, 0)
    return pl.pallas_call(
        functools.partial(_combine_kernel, rows=rows),
        grid_spec=pltpu.PrefetchScalarGridSpec(
            num_scalar_prefetch=1,
            grid=(T // rows,),
            in_specs=[pl.BlockSpec((rows, D), lambda i, ps: (i, 0)),
                      pl.BlockSpec((rows, TOP_K), lambda i, ps: (i, 0)),
                      pl.BlockSpec(memory_space=pl.ANY), pl.BlockSpec(memory_space=pl.ANY),
                      pl.BlockSpec(memory_space=pl.ANY),
                      pl.BlockSpec((D, ds), full), pl.BlockSpec((D, ds), full), pl.BlockSpec((ds, D), full),
                      pl.BlockSpec((1, D), full), pl.BlockSpec((1, D), full)],
            out_specs=pl.BlockSpec((rows, D), lambda i, ps: (i, 0)),
            scratch_shapes=[pltpu.SMEM((n,), jnp.int32), pltpu.SMEM((n,), jnp.int32),
                            pltpu.VMEM((n, D), F32),
                            pltpu.SemaphoreType.DMA((2,)), pltpu.SemaphoreType.DMA(())]),
        out_shape=jax.ShapeDtypeStruct((T, D), F32),
        compiler_params=_cparams("arbitrary"),
        name="moe_combine",
    )(pstart, h1, wk_t, e_flat, p_flat, ys, sh_gate, sh_up, sh_down, g, b)


def _moe(h1, router_w, router_b, w_gate, w_up, w_down, sh_gate, sh_up, sh_down, ln_g, ln_b):
    T, D = h1.shape
    rw_t = router_w.T
    rw_hi = rw_t.astype(BF16)
    rw_lo = (rw_t - rw_hi.astype(F32)).astype(BF16)
    tri = jnp.asarray(np.triu(np.ones((ROUTE_COLS, ROUTE_COLS), np.float32), 1), BF16)
    eidx, pos, wk, cnt = _route(h1, rw_hi, rw_lo, router_b.reshape(N_EXPERTS, 1), tri)

    bm = EXPERT_ROWS
    counts = cnt[:, 0].astype(jnp.int32)
    padded = (counts + bm - 1) // bm * bm
    pend = jnp.cumsum(padded)
    pstart = (pend - padded).astype(jnp.int32)
    n_blocks = T * TOP_K // bm + N_EXPERTS
    n_active = (pend[-1:] // bm).astype(jnp.int32)
    block_e = jnp.clip(jnp.searchsorted(pend, jnp.arange(n_blocks, dtype=jnp.int32) * bm, side='right'),
                       0, N_EXPERTS - 1).astype(jnp.int32)

    xs_init = jnp.zeros((n_blocks * bm, D), F32)
    xs = _dispatch(h1, _tile_major(eidx, DISPATCH_ROWS), _tile_major(pos, DISPATCH_ROWS), pstart, xs_init)
    ys = _experts(xs, w_gate, w_up, w_down, block_e, n_active)
    return _combine(h1, wk.T, _tile_major(eidx, COMBINE_ROWS), _tile_major(pos, COMBINE_ROWS), pstart, ys,
                    sh_gate.astype(BF16), sh_up.astype(BF16), sh_down.astype(BF16),
                    ln_g.reshape(1, D), ln_b.reshape(1, D))


def _block_diag(w):
    n, c, d = w.shape
    eye = jnp.eye(n, dtype=w.dtype)
    return (eye[:, None, :, None] * w[:, :, None, :]).reshape(n * c, n * d)


def _mixer_and_ln(h_in_x, ln_g, ln_b, w_in, conv_w, conv_b, rg_w_a, rg_b_a, rg_w_x, rg_b_x, rg_lambda,
                  cmp_k, cmp_v, rel_bias, w_out, ln1_g, ln1_b, B, S, first):
    T, D = h_in_x.shape
    d_rnn = D // D_RNN_FRAC
    dh = (D - d_rnn) // N_HEADS
    kvw = N_KV_HEADS * dh
    n_att = N_HEADS * dh + 6 * kvw
    c_rg, c_att = 2 * d_rnn, 2 * d_rnn + n_att
    w_rg = w_in[:, :c_rg].astype(BF16)
    w_att = w_in[:, c_rg:c_att].astype(BF16)
    w_gl = jnp.pad(w_in[:, c_att:], ((0, 0), (0, LANES - 3 * N_HEADS))).astype(BF16)
    if first:
        g0, b0 = ln_g.reshape(1, D), ln_b.reshape(1, D)
    else:
        raise NotImplementedError("depth > 1")
    h, rg, att, gates = _ln_inproj(h_in_x, g0, b0, w_rg, w_att, w_gl, q_cols=N_HEADS * dh, q_scale=dh ** -0.5)

    y_rnn = _rglru(rg.reshape(B, S, c_rg), conv_w, conv_b.reshape(1, d_rnn),
                   _block_diag(rg_w_a).astype(BF16), rg_b_a.reshape(1, d_rnn),
                   _block_diag(rg_w_x).astype(BF16), rg_b_x.reshape(1, d_rnn), rg_lambda.reshape(1, d_rnn))

    nch = S // CMP_STRIDE
    c0 = N_HEADS * dh

    def chunks_of(j):
        a = att[:, c0 + j * kvw:c0 + (j + 1) * kvw].reshape(B, nch, CMP_STRIDE, N_KV_HEADS, dh)
        return a.transpose(0, 3, 1, 2, 4).reshape(B, N_KV_HEADS, nch, CMP_STRIDE * dh)

    chunks = jnp.stack([chunks_of(0), chunks_of(1)])
    half = CMP_STRIDE * dh
    stack = lambda i: jnp.stack([cmp_k[i], cmp_v[i]])
    pos = jnp.broadcast_to(stack(0).reshape(2, 1, CMP_LEN * dh), (2, SUBLANES, CMP_LEN * dh)).astype(BF16)
    w1 = stack(1).astype(BF16)
    kv_cmp = _compress(chunks, pos, w1[:, :half], w1[:, half:], stack(2)[:, None, :],
                       stack(3).astype(BF16), stack(4)[:, None, :])

    y_att = _attention(att, gates, kv_cmp[0], kv_cmp[1], rel_bias, B, S, dh)
    w_o = w_out.astype(BF16)
    return _outproj_ln(h, y_rnn.reshape(T, d_rnn), y_att, w_o[:d_rnn], w_o[d_rnn:],
                       ln1_g.reshape(1, D), ln1_b.reshape(1, D))


def kernel(x, ln_in_g, ln_in_b, w_in, conv_w, conv_b, rg_w_a, rg_b_a, rg_w_x, rg_b_x, rg_lambda, cmp_pos_k, cmp_k_w1, cmp_k_b1, cmp_k_w2, cmp_k_b2, cmp_pos_v, cmp_v_w1, cmp_v_b1, cmp_v_w2, cmp_v_b2, rel_bias, w_out, ln1_g, ln1_b, router_w, router_b, w_gate, w_up, w_down, sh_gate, sh_up, sh_down, ln2_g, ln2_b):
    B, S, D = x.shape
    assert w_in.shape[0] == DEPTH
    l = 0
    cmp_k = (cmp_pos_k[l], cmp_k_w1[l], cmp_k_b1[l], cmp_k_w2[l], cmp_k_b2[l])
    cmp_v = (cmp_pos_v[l], cmp_v_w1[l], cmp_v_b1[l], cmp_v_w2[l], cmp_v_b2[l])
    h1 = _mixer_and_ln(x.reshape(B * S, D), ln_in_g, ln_in_b, w_in[l], conv_w[l], conv_b[l], rg_w_a[l], rg_b_a[l],
                       rg_w_x[l], rg_b_x[l], rg_lambda[l], cmp_k, cmp_v, rel_bias, w_out[l], ln1_g[l], ln1_b[l],
                       B, S, True)
    out = _moe(h1, router_w[l], router_b[l], w_gate[l], w_up[l], w_down[l], sh_gate[l], sh_up[l], sh_down[l],
               ln2_g[l], ln2_b[l])
    return out.reshape(B, S, D)
```

```python
import functools
import math

import numpy as np
import jax
import jax.numpy as jnp
from jax import lax
from jax.experimental import pallas as pl
from jax.experimental.pallas import tpu as pltpu

F32 = jnp.float32
BF16 = jnp.bfloat16

DEPTH = 1
D_RNN_FRAC = 2
RG_BLOCKS = 8
CONV_W = 4
RG_C = 8.0
N_HEADS = 8
N_KV_HEADS = 2
GQA = N_HEADS // N_KV_HEADS
CMP_STRIDE = 16
CMP_LEN = 2 * CMP_STRIDE
SEL_BLOCK = 64
SEL_TOPN = 16
WINDOW = 512
Q_BLOCK = 128
NUM_BUCKETS = 32
MAX_DISTANCE = 128
N_EXPERTS = 256
TOP_K = 8
N_GROUPS = 8
TOPK_GROUPS = 4
ROUTED_SCALE = 2.5
ALPHA = (2 * DEPTH) ** 0.25
LN_EPS = 1e-5
NEG_INF = -1e30
FORCE_SCORE = 1e9

LANES = 128
SUBLANES = 8
VMEM_LIMIT_BYTES = 56 * 1024 * 1024

PROJ_ROWS = 512
RG_ROWS = 256
ROUTE_COLS = 512
DISPATCH_ROWS = 256
EXPERT_ROWS = 256
COMBINE_ROWS = 128
ISSUE_UNROLL = 8
SEL_FAR_KEYS = 512
CMP_NEAR = 32
CMP_PAD = CMP_NEAR - Q_BLOCK // CMP_STRIDE
SEL_NEAR = 2 * Q_BLOCK
KA_WIDTH = 256


def _cparams(*sem):
    return pltpu.CompilerParams(dimension_semantics=sem, vmem_limit_bytes=VMEM_LIMIT_BYTES)


def _sigmoid(x):
    return 1.0 / (1.0 + jnp.exp(-x))


def _gelu_tanh(x):
    return 0.5 * x * (1.0 + jnp.tanh(math.sqrt(2.0 / math.pi) * (x + 0.044715 * (x * x * x))))


def _layer_norm(x, g, b):
    mu = jnp.mean(x, axis=-1, keepdims=True)
    xc = x - mu
    var = jnp.mean(xc * xc, axis=-1, keepdims=True)
    return xc * lax.rsqrt(var + LN_EPS) * g + b


def _dot(a, b):
    return jnp.dot(a, b, preferred_element_type=F32)


def _dot_nt(a, b):
    return lax.dot_general(a, b, (((1,), (1,)), ((), ())), preferred_element_type=F32)


def _ln_inproj_kernel(x_ref, g_ref, b_ref, wrg_ref, watt_ref, wgl_ref,
                      h_ref, rg_ref, att_ref, gate_ref, *, q_cols, q_scale):
    h = _layer_norm(x_ref[...], g_ref[...], b_ref[...])
    h_ref[...] = h
    hb = h.astype(BF16)
    rg_ref[...] = _dot(hb, wrg_ref[...])
    att = _dot(hb, watt_ref[...])
    att_ref[:, :q_cols] = (att[:, :q_cols] * q_scale).astype(BF16)
    att_ref[:, q_cols:] = att[:, q_cols:].astype(BF16)
    gate_ref[...] = _sigmoid(_dot(hb, wgl_ref[...]))


def _ln_inproj(x2, g, b, w_rg, w_att, w_gl, *, q_cols, q_scale):
    T, D = x2.shape
    n_rg, n_att, n_gl = w_rg.shape[1], w_att.shape[1], w_gl.shape[1]
    tm = PROJ_ROWS
    row = lambda i: (i, 0)
    full = lambda i: (0, 0)
    return pl.pallas_call(
        functools.partial(_ln_inproj_kernel, q_cols=q_cols, q_scale=q_scale),
        grid=(T // tm,),
        in_specs=[pl.BlockSpec((tm, D), row), pl.BlockSpec((1, D), full), pl.BlockSpec((1, D), full),
                  pl.BlockSpec((D, n_rg), full), pl.BlockSpec((D, n_att), full), pl.BlockSpec((D, n_gl), full)],
        out_specs=[pl.BlockSpec((tm, D), row), pl.BlockSpec((tm, n_rg), row),
                   pl.BlockSpec((tm, n_att), row), pl.BlockSpec((tm, n_gl), row)],
        out_shape=[jax.ShapeDtypeStruct((T, D), F32), jax.ShapeDtypeStruct((T, n_rg), F32),
                   jax.ShapeDtypeStruct((T, n_att), BF16), jax.ShapeDtypeStruct((T, n_gl), F32)],
        compiler_params=_cparams("arbitrary"),
        name="ln_inproj",
    )(x2, g, b, w_rg, w_att, w_gl)


def _rglru_kernel(rg_ref, cw_ref, cb_ref, wa_ref, ba_ref, wx_ref, bx_ref, lam_ref, y_ref,
                  xprev_scr, hprev_scr, *, d_rnn, rows):
    s = pl.program_id(1)

    @pl.when(s == 0)
    def _():
        xprev_scr[...] = jnp.zeros_like(xprev_scr)
        hprev_scr[...] = jnp.zeros_like(hprev_scr)

    xr = rg_ref[0, :, :d_rnn]
    gr = rg_ref[0, :, d_rnn:]
    xcat = jnp.concatenate([xprev_scr[...], xr], axis=0)
    xc = cw_ref[CONV_W - 1:CONV_W, :] * xr + cb_ref[...]
    for k in range(1, CONV_W):
        shifted = pltpu.roll(xcat, k, 0)[SUBLANES:, :]
        xc = xc + cw_ref[CONV_W - 1 - k:CONV_W - k, :] * shifted
    xprev_scr[...] = xr[rows - SUBLANES:, :]

    xcb = xc.astype(BF16)
    r = _sigmoid(_dot(xcb, wa_ref[...]) + ba_ref[...])
    i = _sigmoid(_dot(xcb, wx_ref[...]) + bx_ref[...])
    neg_lam = -lam_ref[...]
    softplus = jnp.maximum(neg_lam, 0.0) + jnp.log1p(jnp.exp(-jnp.abs(neg_lam)))
    log_a = -RG_C * r * softplus
    a = jnp.exp(log_a)
    u = jnp.sqrt(jnp.tanh(-log_a) * (1.0 + a * a)) * (i * xc)

    t_idx = lax.broadcasted_iota(jnp.int32, (rows, d_rnn), 0)
    d = 1
    while d < rows:
        keep = t_idx >= d
        a_sh = jnp.where(keep, pltpu.roll(a, d, 0), 1.0)
        u_sh = jnp.where(keep, pltpu.roll(u, d, 0), 0.0)
        u = u + a * u_sh
        a = a * a_sh
        d *= 2
    h = u + a * hprev_scr[0:1, :]
    hprev_scr[...] = jnp.broadcast_to(h[rows - 1:rows, :], hprev_scr.shape)
    y_ref[0] = (h * _gelu_tanh(gr)).astype(y_ref.dtype)


def _rglru(rg3, conv_w, conv_b, wa_bd, b_a, wx_bd, b_x, lam):
    B, S, two_d = rg3.shape
    d = two_d // 2
    rows = RG_ROWS
    full = lambda b, s: (0, 0)
    return pl.pallas_call(
        functools.partial(_rglru_kernel, d_rnn=d, rows=rows),
        grid=(B, S // rows),
        in_specs=[pl.BlockSpec((1, rows, two_d), lambda b, s: (b, s, 0)),
                  pl.BlockSpec((CONV_W, d), full), pl.BlockSpec((1, d), full),
                  pl.BlockSpec((d, d), full), pl.BlockSpec((1, d), full),
                  pl.BlockSpec((d, d), full), pl.BlockSpec((1, d), full), pl.BlockSpec((1, d), full)],
        out_specs=pl.BlockSpec((1, rows, d), lambda b, s: (b, s, 0)),
        out_shape=jax.ShapeDtypeStruct((B, S, d), BF16),
        scratch_shapes=[pltpu.VMEM((SUBLANES, d), F32), pltpu.VMEM((SUBLANES, d), F32)],
        compiler_params=_cparams("arbitrary", "arbitrary"),
        name="rglru",
    )(rg3, conv_w, conv_b, wa_bd, b_a, wx_bd, b_x, lam)


def _compress_kernel(x_ref, pos_ref, w1a_ref, w1b_ref, b1_ref, w2_ref, b2_ref, o_ref, *, n_chunks):
    x = x_ref[0, 0, 0]
    w1a = w1a_ref[0]
    w1b = w1b_ref[0]
    pos = pos_ref[0]
    half = x.shape[1]
    pos_term = _dot(pos[:, :half], w1a) + _dot(pos[:, half:], w1b)
    first = _dot(x, w1a)
    second = _dot(x, w1b)
    hidden = first + pltpu.roll(second, n_chunks - 1, 0) + pos_term[0:1, :] + b1_ref[0]
    out = _dot(_gelu_tanh(hidden).astype(BF16), w2_ref[0]) + b2_ref[0]
    rows = lax.broadcasted_iota(jnp.int32, out.shape, 0)
    o_ref[0, 0, 0] = jnp.where(rows < n_chunks - 1, out, 0.0).astype(o_ref.dtype)


def _compress(chunks, pos, w1a, w1b, b1, w2, b2):
    two, B, HKV, NCH, half = chunks.shape
    hid = w1a.shape[-1]
    dh = w2.shape[-1]
    kv = lambda c, b, h: (c, 0, 0)
    return pl.pallas_call(
        functools.partial(_compress_kernel, n_chunks=NCH),
        grid=(two, B, HKV),
        in_specs=[pl.BlockSpec((1, 1, 1, NCH, half), lambda c, b, h: (c, b, h, 0, 0)),
                  pl.BlockSpec((1, SUBLANES, 2 * half), kv),
                  pl.BlockSpec((1, half, hid), kv), pl.BlockSpec((1, half, hid), kv),
                  pl.BlockSpec((1, 1, hid), kv), pl.BlockSpec((1, hid, dh), kv), pl.BlockSpec((1, 1, dh), kv)],
        out_specs=pl.BlockSpec((1, 1, 1, NCH, dh), lambda c, b, h: (c, b, h, 0, 0)),
        out_shape=jax.ShapeDtypeStruct((two, B, HKV, NCH, dh), BF16),
        compiler_params=_cparams("arbitrary", "arbitrary", "arbitrary"),
        name="compress_kv",
    )(chunks, pos, w1a, w1b, b1, w2, b2)


def _t5_bucket_np(dist):
    n = np.maximum(dist, 0)
    max_exact = NUM_BUCKETS // 2
    nf = np.maximum(n, 1).astype(np.float32)
    large = max_exact + (np.log(nf / np.float32(max_exact)) / np.float32(math.log(MAX_DISTANCE / max_exact))
                         * np.float32(NUM_BUCKETS - max_exact)).astype(np.int32)
    return np.where(n < max_exact, n, np.minimum(large, NUM_BUCKETS - 1)).astype(np.int32)


def _bias_table(rel_bias, dist, valid):
    q, k = dist.shape
    tab = rel_bias.astype(F32)[jnp.asarray(_t5_bucket_np(dist))]
    tab = jnp.where(jnp.asarray(valid)[:, :, None], tab, NEG_INF)
    return tab.transpose(2, 0, 1).reshape(N_KV_HEADS, GQA * q, k)


def _nsa_kernel(q_ref, kcp_ref, vcp_ref, ov_ref, ka_ref, vs_ref, kw_ref, vw_ref, gate_ref,
                b31_ref, bcn_ref, bw_ref, bsn_ref, o_ref, qa_scr, m_scr, l_scr, acc_scr, *, dh):
    qb = pl.program_id(2)
    q0 = qb * Q_BLOCK
    rows = GQA * Q_BLOCK
    q4 = q_ref[0, 0, 0]
    b31 = b31_ref[0]
    row_q = lax.broadcasted_iota(jnp.int32, (rows, 1), 0) % Q_BLOCK

    n_cp = kcp_ref.shape[2]
    cpb = Q_BLOCK // CMP_STRIDE
    s_far = _dot_nt(q4, kcp_ref[0, 0]) + b31
    col = lax.broadcasted_iota(jnp.int32, (rows, n_cp), 1)
    s_far = jnp.where((col >= CMP_PAD) & (col < cpb * qb), s_far, NEG_INF)
    near0 = pl.multiple_of(cpb * qb, SUBLANES)
    s_near = _dot_nt(q4, kcp_ref[0, 0, pl.ds(near0, CMP_NEAR), :]) + bcn_ref[0]
    coln = lax.broadcasted_iota(jnp.int32, (rows, CMP_NEAR), 1)
    s_near = jnp.where(coln + cpb * qb >= CMP_PAD, s_near, NEG_INF)
    m_c = jnp.maximum(jnp.max(s_far, axis=1, keepdims=True), jnp.max(s_near, axis=1, keepdims=True))
    p_far = jnp.exp(s_far - m_c)
    p_near = jnp.exp(s_near - m_c)
    l_c = jnp.sum(p_far, axis=1, keepdims=True) + jnp.sum(p_near, axis=1, keepdims=True)
    any_valid = (q0 + row_q >= CMP_LEN - 1).astype(F32)
    scale_c = any_valid / l_c
    pf_hi = p_far.astype(BF16)
    pn_hi = p_near.astype(BF16)
    vc_near = vcp_ref[0, 0, pl.ds(near0, CMP_NEAR), :]
    o_c = (_dot(pf_hi, vcp_ref[0, 0]) + _dot(pn_hi, vc_near)) * scale_c
    pf_lo = (p_far - pf_hi.astype(F32)).astype(BF16)
    pn_lo = (p_near - pn_hi.astype(F32)).astype(BF16)
    ov_near = ov_ref[pl.ds(near0, CMP_NEAR), :]
    imp4 = (_dot(pf_hi, ov_ref[...]) + _dot(pf_lo, ov_ref[...])
            + _dot(pn_hi, ov_near) + _dot(pn_lo, ov_near)) * scale_c
    imp = imp4[0:Q_BLOCK]
    for g in range(1, GQA):
        imp = imp + imp4[g * Q_BLOCK:(g + 1) * Q_BLOCK]

    n_sb = imp.shape[1]
    blk = lax.broadcasted_iota(jnp.int32, (Q_BLOCK, n_sb), 1)
    qq = lax.broadcasted_iota(jnp.int32, (Q_BLOCK, n_sb), 0)
    cur = (q0 + qq) // SEL_BLOCK
    forced = (blk == 0) | (blk == cur) | (blk == cur - 1)
    score = jnp.where(forced, FORCE_SCORE, jnp.where(blk <= cur, imp, -1.0))
    v_t = score.T
    r_t = lax.broadcasted_iota(jnp.int32, (n_sb, Q_BLOCK), 0)
    sel_t = jnp.zeros((n_sb, Q_BLOCK), F32)
    for _ in range(min(SEL_TOPN, n_sb)):
        mx = jnp.max(v_t, axis=0, keepdims=True)
        idx = jnp.min(jnp.where(v_t == mx, r_t, n_sb), axis=0, keepdims=True)
        hit = r_t == idx
        sel_t = jnp.where(hit, 1.0, sel_t)
        v_t = jnp.where(hit, -jnp.inf, v_t)
    sel = sel_t.T
    near_blk0 = 2 * qb - SEL_NEAR // SEL_BLOCK // 2
    neg_near = jnp.where(sel > 0.5, 0.0, NEG_INF)
    neg_far = jnp.where(blk < near_blk0, neg_near, NEG_INF)

    qa_scr[:, n_sb:n_sb + dh] = q4
    qa_scr[:, n_sb + dh:] = jnp.zeros((rows, KA_WIDTH - n_sb - dh), BF16)
    for g in range(GQA):
        qa_scr[g * Q_BLOCK:(g + 1) * Q_BLOCK, :n_sb] = neg_near.astype(BF16)
    ks0 = pl.multiple_of(q0, Q_BLOCK)
    s_n = _dot_nt(qa_scr[...], ka_ref[0, 0, pl.ds(ks0, SEL_NEAR), :]) + bsn_ref[0]
    colk = lax.broadcasted_iota(jnp.int32, (rows, SEL_NEAR), 1)
    s_n = jnp.where(colk + q0 >= SEL_NEAR // 2, s_n, NEG_INF)
    m0 = jnp.max(s_n, axis=1, keepdims=True)
    p_n = jnp.exp(s_n - m0)
    m_scr[...] = m0
    l_scr[...] = jnp.sum(p_n, axis=1, keepdims=True)
    acc_scr[...] = _dot(p_n.astype(BF16), vs_ref[0, 0, pl.ds(ks0, SEL_NEAR), :])

    for g in range(GQA):
        qa_scr[g * Q_BLOCK:(g + 1) * Q_BLOCK, :n_sb] = neg_far.astype(BF16)
    n_far = jnp.maximum(q0 - SEL_NEAR // 2 + SEL_FAR_KEYS - 1, 0) // SEL_FAR_KEYS

    def far_body(t, carry):
        k0 = pl.multiple_of(SEL_NEAR // 2 + t * SEL_FAR_KEYS, SEL_NEAR // 2)
        s_f = _dot_nt(qa_scr[...], ka_ref[0, 0, pl.ds(k0, SEL_FAR_KEYS), :]) + b31
        m_old = m_scr[...]
        m_new = jnp.maximum(m_old, jnp.max(s_f, axis=1, keepdims=True))
        alpha = jnp.exp(m_old - m_new)
        p_f = jnp.exp(s_f - m_new)
        l_scr[...] = alpha * l_scr[...] + jnp.sum(p_f, axis=1, keepdims=True)
        acc_scr[...] = alpha * acc_scr[...] + _dot(p_f.astype(BF16), vs_ref[0, 0, pl.ds(k0, SEL_FAR_KEYS), :])
        m_scr[...] = m_new
        return carry

    lax.fori_loop(0, n_far, far_body, 0)
    o_s = acc_scr[...] / l_scr[...]

    n_win = WINDOW + Q_BLOCK
    s_w = _dot_nt(q4, kw_ref[0, 0, pl.ds(ks0, n_win), :]) + bw_ref[0]
    colw = lax.broadcasted_iota(jnp.int32, (rows, n_win), 1)
    s_w = jnp.where(colw + q0 >= WINDOW, s_w, NEG_INF)
    m_w = jnp.max(s_w, axis=1, keepdims=True)
    p_w = jnp.exp(s_w - m_w)
    l_w = jnp.sum(p_w, axis=1, keepdims=True)
    o_w = _dot(p_w.astype(BF16), vw_ref[0, 0, pl.ds(ks0, n_win), :]) / l_w

    gates = gate_ref[0, 0]
    for g in range(GQA):
        sl = slice(g * Q_BLOCK, (g + 1) * Q_BLOCK)
        o = (gates[:, 3 * g:3 * g + 1] * o_c[sl] + gates[:, 3 * g + 1:3 * g + 2] * o_s[sl]
             + gates[:, 3 * g + 2:3 * g + 3] * o_w[sl])
        o_ref[0, 0, 0, sl, :] = o.astype(o_ref.dtype)


def _nsa(q_r, kc_pad, vc_pad, ov_pad, ka_pad, vs_pad, kw_pad, vw_pad, gates_r, b31, bcn, bw, bsn):
    B, HKV, NQB, rows, dh = q_r.shape
    n_sb = ov_pad.shape[1]
    per_bh = lambda b, h, i: (b, h, 0, 0)
    per_h = lambda b, h, i: (h, 0, 0)
    return pl.pallas_call(
        functools.partial(_nsa_kernel, dh=dh),
        grid=(B, HKV, NQB),
        in_specs=[pl.BlockSpec((1, 1, 1, rows, dh), lambda b, h, i: (b, h, i, 0, 0)),
                  pl.BlockSpec((1, 1) + kc_pad.shape[2:], per_bh),
                  pl.BlockSpec((1, 1) + vc_pad.shape[2:], per_bh),
                  pl.BlockSpec(ov_pad.shape, lambda b, h, i: (0, 0)),
                  pl.BlockSpec((1, 1) + ka_pad.shape[2:], per_bh),
                  pl.BlockSpec((1, 1) + vs_pad.shape[2:], per_bh),
                  pl.BlockSpec((1, 1) + kw_pad.shape[2:], per_bh),
                  pl.BlockSpec((1, 1) + vw_pad.shape[2:], per_bh),
                  pl.BlockSpec((1, 1, Q_BLOCK, LANES), lambda b, h, i: (b, h, i, 0)),
                  pl.BlockSpec((1,) + b31.shape[1:], per_h),
                  pl.BlockSpec((1,) + bcn.shape[1:], per_h),
                  pl.BlockSpec((1,) + bw.shape[1:], per_h),
                  pl.BlockSpec((1,) + bsn.shape[1:], per_h)],
        out_specs=pl.BlockSpec((1, 1, 1, rows, dh), lambda b, h, i: (b, h, i, 0, 0)),
        out_shape=jax.ShapeDtypeStruct((B, HKV, NQB, rows, dh), BF16),
        scratch_shapes=[pltpu.VMEM((rows, KA_WIDTH), BF16), pltpu.VMEM((rows, 1), F32),
                        pltpu.VMEM((rows, 1), F32), pltpu.VMEM((rows, dh), F32)],
        compiler_params=_cparams("arbitrary", "arbitrary", "arbitrary"),
        name="nsa_attention",
    )(q_r, kc_pad, vc_pad, ov_pad, ka_pad, vs_pad, kw_pad, vw_pad, gates_r, b31, bcn, bw, bsn)


def _attention(att, gates, k_cmp, v_cmp, rel_bias, B, S, dh):
    HKV = N_KV_HEADS
    H = N_HEADS
    nqb = S // Q_BLOCK
    n_sb = S // SEL_BLOCK
    assert n_sb == LANES, "selection blocks are laid out on one lane tile"
    kvw = HKV * dh
    c0 = H * dh

    def kv_heads(j):
        a = att[:, c0 + j * kvw:c0 + (j + 1) * kvw].reshape(B, S, HKV, dh)
        return a.transpose(0, 2, 1, 3)

    q_r = (att[:, :c0].reshape(B, nqb, Q_BLOCK, HKV, GQA, dh).transpose(0, 3, 1, 4, 2, 5)
           .reshape(B, HKV, nqb, GQA * Q_BLOCK, dh))
    ks, vs, kw, vw = kv_heads(2), kv_heads(3), kv_heads(4), kv_heads(5)
    front = lambda a, n: jnp.pad(a, ((0, 0), (0, 0), (n, 0), (0, 0)))
    onehot = np.zeros((S, KA_WIDTH), np.float32)
    onehot[np.arange(S), np.arange(S) // SEL_BLOCK] = 1.0
    ka = jnp.concatenate([jnp.broadcast_to(jnp.asarray(onehot[:, :n_sb], BF16), (B, HKV, S, n_sb)), ks,
                          jnp.zeros((B, HKV, S, KA_WIDTH - n_sb - dh), BF16)], axis=-1)
    ka_pad = front(ka, SEL_NEAR // 2)
    vs_pad = front(vs, SEL_NEAR // 2)
    kw_pad = front(kw, WINDOW)
    vw_pad = front(vw, WINDOW)
    kc_pad = front(k_cmp, CMP_PAD)
    vc_pad = front(v_cmp, CMP_PAD)
    n_c = S // CMP_STRIDE
    c_start = np.arange(n_c)[:, None] * CMP_STRIDE
    s_start = np.arange(n_sb)[None, :] * SEL_BLOCK
    overlap = ((c_start < s_start + SEL_BLOCK) & (c_start + CMP_LEN > s_start)).astype(np.float32)
    overlap[n_c - 1] = 0.0
    ov_pad = jnp.asarray(np.concatenate([np.zeros((CMP_PAD, n_sb), np.float32), overlap]), BF16)

    gates_r = gates[:, :3 * H].reshape(B, S, HKV, 3 * GQA).transpose(0, 2, 1, 3)
    gates_r = jnp.pad(gates_r, ((0, 0), (0, 0), (0, 0), (0, LANES - 3 * GQA)))

    qi = np.arange(Q_BLOCK)[:, None]
    d_cn = qi - CMP_STRIDE * (np.arange(CMP_NEAR)[None, :] - CMP_PAD) - (CMP_LEN - 1)
    d_w = qi + WINDOW - np.arange(WINDOW + Q_BLOCK)[None, :]
    d_sn = qi + SEL_NEAR // 2 - np.arange(SEL_NEAR)[None, :]
    bcn = _bias_table(rel_bias, d_cn, d_cn >= 0)
    bw = _bias_table(rel_bias, d_w, (d_w >= 0) & (d_w < WINDOW))
    bsn = _bias_table(rel_bias, d_sn, d_sn >= 0)
    b31 = jnp.repeat(rel_bias.astype(F32)[NUM_BUCKETS - 1].reshape(HKV, GQA), Q_BLOCK, axis=1)[:, :, None]

    o = _nsa(q_r, kc_pad, vc_pad, ov_pad, ka_pad, vs_pad, kw_pad, vw_pad, gates_r, b31, bcn, bw, bsn)
    o = o.reshape(B, HKV, nqb, GQA, Q_BLOCK, dh).transpose(0, 2, 4, 1, 3, 5)
    return o.reshape(B * S, H * dh)


def _outproj_kernel(h_ref, yr_ref, ya_ref, wr_ref, wa_ref, g_ref, b_ref, o_ref):
    mix = _dot(yr_ref[...], wr_ref[...]) + _dot(ya_ref[...], wa_ref[...])
    o_ref[...] = _layer_norm(ALPHA * h_ref[...] + mix, g_ref[...], b_ref[...])


def _outproj_ln(h, y_rnn, y_att, w_r, w_a, g, b):
    T, D = h.shape
    dr, da = y_rnn.shape[1], y_att.shape[1]
    tm = PROJ_ROWS
    row = lambda i: (i, 0)
    full = lambda i: (0, 0)
    return pl.pallas_call(
        _outproj_kernel,
        grid=(T // tm,),
        in_specs=[pl.BlockSpec((tm, D), row), pl.BlockSpec((tm, dr), row), pl.BlockSpec((tm, da), row),
                  pl.BlockSpec((dr, D), full), pl.BlockSpec((da, D), full),
                  pl.BlockSpec((1, D), full), pl.BlockSpec((1, D), full)],
        out_specs=pl.BlockSpec((tm, D), row),
        out_shape=jax.ShapeDtypeStruct((T, D), F32),
        compiler_params=_cparams("arbitrary"),
        name="outproj_ln",
    )(h, y_rnn, y_att, w_r, w_a, g, b)


def _route_kernel(h_ref, wh_ref, wl_ref, rb_ref, tri_ref, e_ref, p_ref, w_ref, cnt_ref, carry_scr, *, cols):
    i = pl.program_id(0)

    @pl.when(i == 0)
    def _():
        carry_scr[...] = jnp.zeros_like(carry_scr)

    h = h_ref[...]
    h_hi = h.astype(BF16)
    h_lo = (h - h_hi.astype(F32)).astype(BF16)
    logits = _dot_nt(wh_ref[...], h_hi) + (_dot_nt(wl_ref[...], h_hi) + _dot_nt(wh_ref[...], h_lo))
    scores = _sigmoid(logits)
    biased = scores + rb_ref[...]
    gsz = N_EXPERTS // N_GROUPS
    gscore = []
    for g in range(N_GROUPS):
        blk = biased[g * gsz:(g + 1) * gsz]
        m1 = jnp.max(blk, axis=0, keepdims=True)
        is_m1 = blk == m1
        n_m1 = jnp.sum(jnp.where(is_m1, 1.0, 0.0), axis=0, keepdims=True)
        m2 = jnp.max(jnp.where(is_m1, -jnp.inf, blk), axis=0, keepdims=True)
        gscore.append(m1 + jnp.where(n_m1 > 1.5, m1, m2))
    masked = []
    for g in range(N_GROUPS):
        rank = jnp.zeros_like(gscore[g])
        for o in range(N_GROUPS):
            if o == g:
                continue
            ahead = (gscore[o] > gscore[g]) | (gscore[o] == gscore[g]) if o < g else gscore[o] > gscore[g]
            rank = rank + jnp.where(ahead, 1.0, 0.0)
        keep = rank < TOPK_GROUPS - 0.5
        masked.append(jnp.where(keep, biased[g * gsz:(g + 1) * gsz], NEG_INF))
    v = jnp.concatenate(masked, axis=0)
    rows = lax.broadcasted_iota(jnp.int32, (N_EXPERTS, cols), 0)
    sel = jnp.zeros((N_EXPERTS, cols), F32)
    idxs, svals = [], []
    for _ in range(TOP_K):
        mx = jnp.max(v, axis=0, keepdims=True)
        idx = jnp.min(jnp.where(v == mx, rows, N_EXPERTS), axis=0, keepdims=True)
        hit = rows == idx
        idxs.append(idx)
        svals.append(jnp.sum(jnp.where(hit, scores, 0.0), axis=0, keepdims=True))
        sel = jnp.where(hit, 1.0, sel)
        v = jnp.where(hit, -jnp.inf, v)
    total = svals[0]
    for k in range(1, TOP_K):
        total = total + svals[k]
    prefix = _dot(sel.astype(BF16), tri_ref[...]) + carry_scr[:, 0:1]
    for k in range(TOP_K):
        hit = rows == idxs[k]
        e_ref[k:k + 1, :] = idxs[k]
        p_ref[k:k + 1, :] = jnp.sum(jnp.where(hit, prefix, 0.0), axis=0, keepdims=True).astype(jnp.int32)
        w_ref[k:k + 1, :] = svals[k] / total * ROUTED_SCALE
    new_carry = carry_scr[...] + jnp.sum(sel, axis=1, keepdims=True)
    carry_scr[...] = new_carry
    cnt_ref[...] = new_carry


def _route(h1, rw_hi, rw_lo, rb, tri):
    T, D = h1.shape
    cols = ROUTE_COLS
    full = lambda i: (0, 0)
    out_col = pl.BlockSpec((TOP_K, cols), lambda i: (0, i))
    return pl.pallas_call(
        functools.partial(_route_kernel, cols=cols),
        grid=(T // cols,),
        in_specs=[pl.BlockSpec((cols, D), lambda i: (i, 0)),
                  pl.BlockSpec((N_EXPERTS, D), full), pl.BlockSpec((N_EXPERTS, D), full),
                  pl.BlockSpec((N_EXPERTS, 1), full), pl.BlockSpec((cols, cols), full)],
        out_specs=[out_col, out_col, out_col, pl.BlockSpec((N_EXPERTS, LANES), full)],
        out_shape=[jax.ShapeDtypeStruct((TOP_K, T), jnp.int32), jax.ShapeDtypeStruct((TOP_K, T), jnp.int32),
                   jax.ShapeDtypeStruct((TOP_K, T), F32), jax.ShapeDtypeStruct((N_EXPERTS, LANES), F32)],
        scratch_shapes=[pltpu.VMEM((N_EXPERTS, LANES), F32)],
        compiler_params=_cparams("arbitrary"),
        name="router",
    )(h1, rw_hi, rw_lo, rb, tri)


def _tile_major(a, tile):
    k, t = a.shape
    return a.reshape(k, t // tile, tile).transpose(1, 0, 2).reshape(-1)


def _load_indices(e_hbm, p_hbm, e_smem, p_smem, idx_sem, n):
    base = pl.multiple_of(pl.program_id(0) * n, n)
    ce = pltpu.make_async_copy(e_hbm.at[pl.ds(base, n)], e_smem, idx_sem.at[0])
    cp = pltpu.make_async_copy(p_hbm.at[pl.ds(base, n)], p_smem, idx_sem.at[1])
    ce.start()
    cp.start()
    ce.wait()
    cp.wait()


def _issue_rows(n, make_copy):
    def body(c, carry):
        for u in range(ISSUE_UNROLL):
            make_copy(c * ISSUE_UNROLL + u).start(priority=u % 2)
        return carry

    lax.fori_loop(0, n // ISSUE_UNROLL, body, 0)


def _token_tile(ref, t):
    return ref.at[pl.ds(pl.multiple_of(t * SUBLANES, SUBLANES), SUBLANES), :]


def _to_token_tiles(dst_ref, x, n):
    for c in range(SUBLANES):
        dst_ref[pl.ds(c, n, stride=SUBLANES), :] = x[:, c * LANES:(c + 1) * LANES]


def _from_token_tiles(src_ref, first, n, c):
    return src_ref[pl.ds(first * SUBLANES + c, n, stride=SUBLANES), :]


def _dispatch_kernel(pstart_ref, h_ref, e_hbm, p_hbm, xs_out, e_smem, p_smem, stage, idx_sem, row_sem, *, rows):
    n = TOP_K * rows
    _load_indices(e_hbm, p_hbm, e_smem, p_smem, idx_sem, n)
    _to_token_tiles(stage, h_ref[...], rows)

    def row_copy(j):
        dst = pstart_ref[e_smem[j]] + p_smem[j]
        return pltpu.make_async_copy(_token_tile(stage, j & (rows - 1)), _token_tile(xs_out, dst), row_sem)

    _issue_rows(n, row_copy)
    whole = xs_out.at[pl.ds(0, n * SUBLANES), :]
    pltpu.make_async_copy(whole, whole, row_sem).wait()


def _dispatch(h1, e_flat, p_flat, pstart, n_rows):
    T, D = h1.shape
    rows = DISPATCH_ROWS
    n = TOP_K * rows
    return pl.pallas_call(
        functools.partial(_dispatch_kernel, rows=rows),
        grid_spec=pltpu.PrefetchScalarGridSpec(
            num_scalar_prefetch=1,
            grid=(T // rows,),
            in_specs=[pl.BlockSpec((rows, D), lambda i, ps: (i, 0)),
                      pl.BlockSpec(memory_space=pl.ANY), pl.BlockSpec(memory_space=pl.ANY)],
            out_specs=pl.BlockSpec(memory_space=pl.ANY),
            scratch_shapes=[pltpu.SMEM((n,), jnp.int32), pltpu.SMEM((n,), jnp.int32),
                            pltpu.VMEM((rows * SUBLANES, LANES), F32),
                            pltpu.SemaphoreType.DMA((2,)), pltpu.SemaphoreType.DMA(())]),
        out_shape=jax.ShapeDtypeStruct((n_rows * SUBLANES, LANES), F32),
        compiler_params=_cparams("arbitrary"),
        name="moe_dispatch",
    )(pstart, h1, e_flat, p_flat)


def _expert_kernel(be_ref, nv_ref, na_ref, x_ref, wg_ref, wu_ref, wd_ref, y_ref, xb, wgb, wub, wdb):
    del na_ref
    i = pl.program_id(0)
    bm = x_ref.shape[0] // SUBLANES

    @pl.when((i == 0) | (be_ref[i] != be_ref[jnp.maximum(i - 1, 0)]))
    def _():
        wgb[...] = wg_ref[0].astype(BF16)
        wub[...] = wu_ref[0].astype(BF16)
        wdb[...] = wd_ref[0].astype(BF16)

    @pl.when(nv_ref[i] > 0)
    def _():
        valid = lax.broadcasted_iota(jnp.int32, (bm, LANES), 0) < nv_ref[i]
        for c in range(SUBLANES):
            x_c = _from_token_tiles(x_ref, 0, bm, c)
            xb[:, c * LANES:(c + 1) * LANES] = jnp.where(valid, x_c, 0.0).astype(BF16)
        x = xb[...]
        g = _dot(x, wgb[...])
        u = _dot(x, wub[...])
        hmid = (g * _sigmoid(g)) * u
        _to_token_tiles(y_ref, _dot(hmid.astype(BF16), wdb[...]), bm)

    @pl.when(nv_ref[i] <= 0)
    def _():
        y_ref[...] = jnp.zeros_like(y_ref)


def _experts(xs, w_gate, w_up, w_down, block_e, n_valid, n_active):
    R = xs.shape[0] // SUBLANES
    E, D, de = w_gate.shape
    bm = EXPERT_ROWS
    x_map = lambda i, be, nv, na: (jnp.minimum(i, jnp.maximum(na[0] - 1, 0)), 0)
    w_map = lambda i, be, nv, na: (be[i], 0, 0)
    return pl.pallas_call(
        _expert_kernel,
        grid_spec=pltpu.PrefetchScalarGridSpec(
            num_scalar_prefetch=3,
            grid=(R // bm,),
            in_specs=[pl.BlockSpec((bm * SUBLANES, LANES), x_map), pl.BlockSpec((1, D, de), w_map),
                      pl.BlockSpec((1, D, de), w_map), pl.BlockSpec((1, de, D), w_map)],
            out_specs=pl.BlockSpec((bm * SUBLANES, LANES), lambda i, be, nv, na: (i, 0)),
            scratch_shapes=[pltpu.VMEM((bm, D), BF16), pltpu.VMEM((D, de), BF16),
                            pltpu.VMEM((D, de), BF16), pltpu.VMEM((de, D), BF16)]),
        out_shape=jax.ShapeDtypeStruct((R * SUBLANES, LANES), F32),
        compiler_params=_cparams("arbitrary"),
        name="moe_experts",
    )(block_e, n_valid, n_active, xs, w_gate, w_up, w_down)


def _combine_kernel(pstart_ref, h_ref, wk_ref, e_hbm, p_hbm, ys_hbm, sg_ref, su_ref, sd_ref, g_ref, b_ref,
                    o_ref, e_smem, p_smem, buf, idx_sem, row_sem, *, rows):
    n = TOP_K * rows
    _load_indices(e_hbm, p_hbm, e_smem, p_smem, idx_sem, n)

    def row_copy(j):
        src = pstart_ref[e_smem[j]] + p_smem[j]
        return pltpu.make_async_copy(_token_tile(ys_hbm, src), _token_tile(buf, j), row_sem)

    _issue_rows(n, row_copy)
    h = h_ref[...]
    hb = h.astype(BF16)
    g = _dot(hb, sg_ref[...])
    u = _dot(hb, su_ref[...])
    shared = _dot(((g * _sigmoid(g)) * u).astype(BF16), sd_ref[...])
    pltpu.make_async_copy(ys_hbm.at[pl.ds(0, n * SUBLANES), :], buf, row_sem).wait()
    wk = wk_ref[...]
    cols = []
    for c in range(SUBLANES):
        acc = wk[:, 0:1] * _from_token_tiles(buf, 0, rows, c)
        for k in range(1, TOP_K):
            acc = acc + wk[:, k:k + 1] * _from_token_tiles(buf, k * rows, rows, c)
        cols.append(acc)
    routed = jnp.concatenate(cols, axis=1)
    o_ref[...] = _layer_norm(ALPHA * h + (routed + shared), g_ref[...], b_ref[...])


def _combine(h1, wk_t, e_flat, p_flat, pstart, ys, sh_gate, sh_up, sh_down, g, b):
    T, D = h1.shape
    ds = sh_gate.shape[1]
    rows = COMBINE_ROWS
    n = TOP_K * rows
    full = lambda i, ps: (0, 0)
    return pl.pallas_call(
        functools.partial(_combine_kernel, rows=rows),
        grid_spec=pltpu.PrefetchScalarGridSpec(
            num_scalar_prefetch=1,
            grid=(T // rows,),
            in_specs=[pl.BlockSpec((rows, D), lambda i, ps: (i, 0)),
                      pl.BlockSpec((rows, TOP_K), lambda i, ps: (i, 0)),
                      pl.BlockSpec(memory_space=pl.ANY), pl.BlockSpec(memory_space=pl.ANY),
                      pl.BlockSpec(memory_space=pl.ANY),
                      pl.BlockSpec((D, ds), full), pl.BlockSpec((D, ds), full), pl.BlockSpec((ds, D), full),
                      pl.BlockSpec((1, D), full), pl.BlockSpec((1, D), full)],
            out_specs=pl.BlockSpec((rows, D), lambda i, ps: (i, 0)),
            scratch_shapes=[pltpu.SMEM((n,), jnp.int32), pltpu.SMEM((n,), jnp.int32),
                            pltpu.VMEM((n * SUBLANES, LANES), F32),
                            pltpu.SemaphoreType.DMA((2,)), pltpu.SemaphoreType.DMA(())]),
        out_shape=jax.ShapeDtypeStruct((T, D), F32),
        compiler_params=_cparams("arbitrary"),
        name="moe_combine",
    )(pstart, h1, wk_t, e_flat, p_flat, ys, sh_gate, sh_up, sh_down, g, b)


def _moe(h1, router_w, router_b, w_gate, w_up, w_down, sh_gate, sh_up, sh_down, ln_g, ln_b):
    T, D = h1.shape
    assert D == SUBLANES * LANES, "a token row is moved as one (SUBLANES, LANES) tile"
    rw_t = router_w.T
    rw_hi = rw_t.astype(BF16)
    rw_lo = (rw_t - rw_hi.astype(F32)).astype(BF16)
    tri = jnp.asarray(np.triu(np.ones((ROUTE_COLS, ROUTE_COLS), np.float32), 1), BF16)
    eidx, pos, wk, cnt = _route(h1, rw_hi, rw_lo, router_b.reshape(N_EXPERTS, 1), tri)

    bm = EXPERT_ROWS
    counts = cnt[:, 0].astype(jnp.int32)
    padded = (counts + bm - 1) // bm * bm
    pend = jnp.cumsum(padded)
    pstart = (pend - padded).astype(jnp.int32)
    n_blocks = T * TOP_K // bm + N_EXPERTS
    n_active = (pend[-1:] // bm).astype(jnp.int32)
    block_e = jnp.clip(jnp.searchsorted(pend, jnp.arange(n_blocks, dtype=jnp.int32) * bm, side='right'),
                       0, N_EXPERTS - 1).astype(jnp.int32)

    n_valid = jnp.clip(counts[block_e] - (jnp.arange(n_blocks, dtype=jnp.int32) * bm - pstart[block_e]), 0, bm)
    xs = _dispatch(h1, _tile_major(eidx, DISPATCH_ROWS), _tile_major(pos, DISPATCH_ROWS), pstart, n_blocks * bm)
    ys = _experts(xs, w_gate, w_up, w_down, block_e, n_valid.astype(jnp.int32), n_active)
    return _combine(h1, wk.T, _tile_major(eidx, COMBINE_ROWS), _tile_major(pos, COMBINE_ROWS), pstart, ys,
                    sh_gate.astype(BF16), sh_up.astype(BF16), sh_down.astype(BF16),
                    ln_g.reshape(1, D), ln_b.reshape(1, D))


def _block_diag(w):
    n, c, d = w.shape
    eye = jnp.eye(n, dtype=w.dtype)
    return (eye[:, None, :, None] * w[:, :, None, :]).reshape(n * c, n * d)


def _mixer_and_ln(h_in_x, ln_g, ln_b, w_in, conv_w, conv_b, rg_w_a, rg_b_a, rg_w_x, rg_b_x, rg_lambda,
                  cmp_k, cmp_v, rel_bias, w_out, ln1_g, ln1_b, B, S, first):
    T, D = h_in_x.shape
    d_rnn = D // D_RNN_FRAC
    dh = (D - d_rnn) // N_HEADS
    kvw = N_KV_HEADS * dh
    n_att = N_HEADS * dh + 6 * kvw
    c_rg, c_att = 2 * d_rnn, 2 * d_rnn + n_att
    w_rg = w_in[:, :c_rg].astype(BF16)
    w_att = w_in[:, c_rg:c_att].astype(BF16)
    w_gl = jnp.pad(w_in[:, c_att:], ((0, 0), (0, LANES - 3 * N_HEADS))).astype(BF16)
    if first:
        g0, b0 = ln_g.reshape(1, D), ln_b.reshape(1, D)
    else:
        raise NotImplementedError("depth > 1")
    h, rg, att, gates = _ln_inproj(h_in_x, g0, b0, w_rg, w_att, w_gl, q_cols=N_HEADS * dh, q_scale=dh ** -0.5)

    y_rnn = _rglru(rg.reshape(B, S, c_rg), conv_w, conv_b.reshape(1, d_rnn),
                   _block_diag(rg_w_a).astype(BF16), rg_b_a.reshape(1, d_rnn),
                   _block_diag(rg_w_x).astype(BF16), rg_b_x.reshape(1, d_rnn), rg_lambda.reshape(1, d_rnn))

    nch = S // CMP_STRIDE
    c0 = N_HEADS * dh

    def chunks_of(j):
        a = att[:, c0 + j * kvw:c0 + (j + 1) * kvw].reshape(B, nch, CMP_STRIDE, N_KV_HEADS, dh)
        return a.transpose(0, 3, 1, 2, 4).reshape(B, N_KV_HEADS, nch, CMP_STRIDE * dh)

    chunks = jnp.stack([chunks_of(0), chunks_of(1)])
    half = CMP_STRIDE * dh
    stack = lambda i: jnp.stack([cmp_k[i], cmp_v[i]])
    pos = jnp.broadcast_to(stack(0).reshape(2, 1, CMP_LEN * dh), (2, SUBLANES, CMP_LEN * dh)).astype(BF16)
    w1 = stack(1).astype(BF16)
    kv_cmp = _compress(chunks, pos, w1[:, :half], w1[:, half:], stack(2)[:, None, :],
                       stack(3).astype(BF16), stack(4)[:, None, :])

    y_att = _attention(att, gates, kv_cmp[0], kv_cmp[1], rel_bias, B, S, dh)
    w_o = w_out.astype(BF16)
    return _outproj_ln(h, y_rnn.reshape(T, d_rnn), y_att, w_o[:d_rnn], w_o[d_rnn:],
                       ln1_g.reshape(1, D), ln1_b.reshape(1, D))


def kernel(x, ln_in_g, ln_in_b, w_in, conv_w, conv_b, rg_w_a, rg_b_a, rg_w_x, rg_b_x, rg_lambda, cmp_pos_k, cmp_k_w1, cmp_k_b1, cmp_k_w2, cmp_k_b2, cmp_pos_v, cmp_v_w1, cmp_v_b1, cmp_v_w2, cmp_v_b2, rel_bias, w_out, ln1_g, ln1_b, router_w, router_b, w_gate, w_up, w_down, sh_gate, sh_up, sh_down, ln2_g, ln2_b):
    B, S, D = x.shape
    assert w_in.shape[0] == DEPTH
    l = 0
    cmp_k = (cmp_pos_k[l], cmp_k_w1[l], cmp_k_b1[l], cmp_k_w2[l], cmp_k_b2[l])
    cmp_v = (cmp_pos_v[l], cmp_v_w1[l], cmp_v_b1[l], cmp_v_w2[l], cmp_v_b2[l])
    h1 = _mixer_and_ln(x.reshape(B * S, D), ln_in_g, ln_in_b, w_in[l], conv_w[l], conv_b[l], rg_w_a[l], rg_b_a[l],
                       rg_w_x[l], rg_b_x[l], rg_lambda[l], cmp_k, cmp_v, rel_bias, w_out[l], ln1_g[l], ln1_b[l],
                       B, S, True)
    out = _moe(h1, router_w[l], router_b[l], w_gate[l], w_up[l], w_down[l], sh_gate[l], sh_up[l], sh_down[l],
               ln2_g[l], ln2_b[l])
    return out.reshape(B, S, D)
```

```python
import functools
import math

import numpy as np
import jax
import jax.numpy as jnp
from jax import lax
from jax.experimental import pallas as pl
from jax.experimental.pallas import tpu as pltpu

F32 = jnp.float32
BF16 = jnp.bfloat16

DEPTH = 1
D_RNN_FRAC = 2
RG_BLOCKS = 8
CONV_W = 4
RG_C = 8.0
N_HEADS = 8
N_KV_HEADS = 2
GQA = N_HEADS // N_KV_HEADS
CMP_STRIDE = 16
CMP_LEN = 2 * CMP_STRIDE
SEL_BLOCK = 64
SEL_TOPN = 16
WINDOW = 512
Q_BLOCK = 128
NUM_BUCKETS = 32
MAX_DISTANCE = 128
N_EXPERTS = 256
TOP_K = 8
N_GROUPS = 8
TOPK_GROUPS = 4
ROUTED_SCALE = 2.5
ALPHA = (2 * DEPTH) ** 0.25
LN_EPS = 1e-5
NEG_INF = -1e30
FORCE_SCORE = 1e9

LANES = 128
SUBLANES = 8
VMEM_LIMIT_BYTES = 56 * 1024 * 1024

PROJ_ROWS = 512
RG_ROWS = 256
ROUTE_COLS = 512
DISPATCH_ROWS = 256
EXPERT_ROWS = 256
COMBINE_ROWS = 128
ISSUE_UNROLL = 8
SEL_FAR_KEYS = 512
CMP_NEAR = 32
CMP_PAD = CMP_NEAR - Q_BLOCK // CMP_STRIDE
SEL_NEAR = 2 * Q_BLOCK
KA_WIDTH = 256
BIAS_PIECES = 3
BIAS_ROWS = 16
V_ROWS = 80


def _cparams(*sem):
    return pltpu.CompilerParams(dimension_semantics=sem, vmem_limit_bytes=VMEM_LIMIT_BYTES)


def _sigmoid(x):
    return 1.0 / (1.0 + jnp.exp(-x))


def _gelu_tanh(x):
    return 0.5 * x * (1.0 + jnp.tanh(math.sqrt(2.0 / math.pi) * (x + 0.044715 * (x * x * x))))


def _layer_norm(x, g, b):
    mu = jnp.mean(x, axis=-1, keepdims=True)
    xc = x - mu
    var = jnp.mean(xc * xc, axis=-1, keepdims=True)
    return xc * lax.rsqrt(var + LN_EPS) * g + b


def _dot(a, b):
    return jnp.dot(a, b, preferred_element_type=F32)


def _dot_nt(a, b):
    return lax.dot_general(a, b, (((1,), (1,)), ((), ())), preferred_element_type=F32)


def _ln_inproj_kernel(x_ref, g_ref, b_ref, wrg_ref, watt_ref, wgl_ref,
                      h_ref, rg_ref, att_ref, gate_ref, *, q_cols, q_scale):
    h = _layer_norm(x_ref[...], g_ref[...], b_ref[...])
    h_ref[...] = h
    hb = h.astype(BF16)
    rg_ref[...] = _dot(hb, wrg_ref[...])
    att = _dot(hb, watt_ref[...])
    att_ref[:, :q_cols] = (att[:, :q_cols] * q_scale).astype(BF16)
    att_ref[:, q_cols:] = att[:, q_cols:].astype(BF16)
    gate_ref[...] = _sigmoid(_dot(hb, wgl_ref[...]))


def _ln_inproj(x2, g, b, w_rg, w_att, w_gl, *, q_cols, q_scale):
    T, D = x2.shape
    n_rg, n_att, n_gl = w_rg.shape[1], w_att.shape[1], w_gl.shape[1]
    tm = PROJ_ROWS
    row = lambda i: (i, 0)
    full = lambda i: (0, 0)
    return pl.pallas_call(
        functools.partial(_ln_inproj_kernel, q_cols=q_cols, q_scale=q_scale),
        grid=(T // tm,),
        in_specs=[pl.BlockSpec((tm, D), row), pl.BlockSpec((1, D), full), pl.BlockSpec((1, D), full),
                  pl.BlockSpec((D, n_rg), full), pl.BlockSpec((D, n_att), full), pl.BlockSpec((D, n_gl), full)],
        out_specs=[pl.BlockSpec((tm, D), row), pl.BlockSpec((tm, n_rg), row),
                   pl.BlockSpec((tm, n_att), row), pl.BlockSpec((tm, n_gl), row)],
        out_shape=[jax.ShapeDtypeStruct((T, D), F32), jax.ShapeDtypeStruct((T, n_rg), F32),
                   jax.ShapeDtypeStruct((T, n_att), BF16), jax.ShapeDtypeStruct((T, n_gl), F32)],
        compiler_params=_cparams("arbitrary"),
        name="ln_inproj",
    )(x2, g, b, w_rg, w_att, w_gl)


def _rglru_kernel(rg_ref, cw_ref, cb_ref, wa_ref, ba_ref, wx_ref, bx_ref, lam_ref, y_ref,
                  xprev_scr, hprev_scr, *, d_rnn, rows):
    s = pl.program_id(1)

    @pl.when(s == 0)
    def _():
        xprev_scr[...] = jnp.zeros_like(xprev_scr)
        hprev_scr[...] = jnp.zeros_like(hprev_scr)

    xr = rg_ref[0, :, :d_rnn]
    gr = rg_ref[0, :, d_rnn:]
    xcat = jnp.concatenate([xprev_scr[...], xr], axis=0)
    xc = cw_ref[CONV_W - 1:CONV_W, :] * xr + cb_ref[...]
    for k in range(1, CONV_W):
        shifted = pltpu.roll(xcat, k, 0)[SUBLANES:, :]
        xc = xc + cw_ref[CONV_W - 1 - k:CONV_W - k, :] * shifted
    xprev_scr[...] = xr[rows - SUBLANES:, :]

    xcb = xc.astype(BF16)
    r = _sigmoid(_dot(xcb, wa_ref[...]) + ba_ref[...])
    i = _sigmoid(_dot(xcb, wx_ref[...]) + bx_ref[...])
    neg_lam = -lam_ref[...]
    softplus = jnp.maximum(neg_lam, 0.0) + jnp.log1p(jnp.exp(-jnp.abs(neg_lam)))
    log_a = -RG_C * r * softplus
    a = jnp.exp(log_a)
    u = jnp.sqrt(jnp.tanh(-log_a) * (1.0 + a * a)) * (i * xc)

    t_idx = lax.broadcasted_iota(jnp.int32, (rows, d_rnn), 0)
    d = 1
    while d < rows:
        keep = t_idx >= d
        a_sh = jnp.where(keep, pltpu.roll(a, d, 0), 1.0)
        u_sh = jnp.where(keep, pltpu.roll(u, d, 0), 0.0)
        u = u + a * u_sh
        a = a * a_sh
        d *= 2
    h = u + a * hprev_scr[0:1, :]
    hprev_scr[...] = jnp.broadcast_to(h[rows - 1:rows, :], hprev_scr.shape)
    y_ref[0] = (h * _gelu_tanh(gr)).astype(y_ref.dtype)


def _rglru(rg3, conv_w, conv_b, wa_bd, b_a, wx_bd, b_x, lam):
    B, S, two_d = rg3.shape
    d = two_d // 2
    rows = RG_ROWS
    full = lambda b, s: (0, 0)
    return pl.pallas_call(
        functools.partial(_rglru_kernel, d_rnn=d, rows=rows),
        grid=(B, S // rows),
        in_specs=[pl.BlockSpec((1, rows, two_d), lambda b, s: (b, s, 0)),
                  pl.BlockSpec((CONV_W, d), full), pl.BlockSpec((1, d), full),
                  pl.BlockSpec((d, d), full), pl.BlockSpec((1, d), full),
                  pl.BlockSpec((d, d), full), pl.BlockSpec((1, d), full), pl.BlockSpec((1, d), full)],
        out_specs=pl.BlockSpec((1, rows, d), lambda b, s: (b, s, 0)),
        out_shape=jax.ShapeDtypeStruct((B, S, d), BF16),
        scratch_shapes=[pltpu.VMEM((SUBLANES, d), F32), pltpu.VMEM((SUBLANES, d), F32)],
        compiler_params=_cparams("arbitrary", "arbitrary"),
        name="rglru",
    )(rg3, conv_w, conv_b, wa_bd, b_a, wx_bd, b_x, lam)


def _compress_kernel(x_ref, pos_ref, w1a_ref, w1b_ref, b1_ref, w2_ref, b2_ref, o_ref, *, n_chunks):
    x = x_ref[0, 0, 0]
    w1a = w1a_ref[0]
    w1b = w1b_ref[0]
    pos = pos_ref[0]
    half = x.shape[1]
    pos_term = _dot(pos[:, :half], w1a) + _dot(pos[:, half:], w1b)
    first = _dot(x, w1a)
    second = _dot(x, w1b)
    hidden = first + pltpu.roll(second, n_chunks - 1, 0) + pos_term[0:1, :] + b1_ref[0]
    out = _dot(_gelu_tanh(hidden).astype(BF16), w2_ref[0]) + b2_ref[0]
    rows = lax.broadcasted_iota(jnp.int32, out.shape, 0)
    o_ref[0, 0, 0] = jnp.where(rows < n_chunks - 1, out, 0.0).astype(o_ref.dtype)


def _compress(chunks, pos, w1a, w1b, b1, w2, b2):
    two, B, HKV, NCH, half = chunks.shape
    hid = w1a.shape[-1]
    dh = w2.shape[-1]
    kv = lambda c, b, h: (c, 0, 0)
    return pl.pallas_call(
        functools.partial(_compress_kernel, n_chunks=NCH),
        grid=(two, B, HKV),
        in_specs=[pl.BlockSpec((1, 1, 1, NCH, half), lambda c, b, h: (c, b, h, 0, 0)),
                  pl.BlockSpec((1, SUBLANES, 2 * half), kv),
                  pl.BlockSpec((1, half, hid), kv), pl.BlockSpec((1, half, hid), kv),
                  pl.BlockSpec((1, 1, hid), kv), pl.BlockSpec((1, hid, dh), kv), pl.BlockSpec((1, 1, dh), kv)],
        out_specs=pl.BlockSpec((1, 1, 1, NCH, dh), lambda c, b, h: (c, b, h, 0, 0)),
        out_shape=jax.ShapeDtypeStruct((two, B, HKV, NCH, dh), BF16),
        compiler_params=_cparams("arbitrary", "arbitrary", "arbitrary"),
        name="compress_kv",
    )(chunks, pos, w1a, w1b, b1, w2, b2)


def _t5_bucket_np(dist):
    n = np.maximum(dist, 0)
    max_exact = NUM_BUCKETS // 2
    nf = np.maximum(n, 1).astype(np.float32)
    large = max_exact + (np.log(nf / np.float32(max_exact)) / np.float32(math.log(MAX_DISTANCE / max_exact))
                         * np.float32(NUM_BUCKETS - max_exact)).astype(np.int32)
    return np.where(n < max_exact, n, np.minimum(large, NUM_BUCKETS - 1)).astype(np.int32)


def _bias_table(rel_bias, offset, n_rows, lo, hi, row_step=1):
    R = n_rows * row_step
    P = R + Q_BLOCK - 1
    d = np.arange(P) + (offset - (R - 1))
    f = rel_bias.astype(F32)[jnp.asarray(_t5_bucket_np(d))]
    f = jnp.where(jnp.asarray((d >= lo) & (d < hi))[:, None], f, NEG_INF).T
    hankel = jnp.tile(f, (1, R + 1))[:, :R * (P + 1)].reshape(N_HEADS, R, P + 1)[:, :, :Q_BLOCK]
    tab = hankel[:, ::-1][:, ::row_step]
    return (tab.reshape(N_KV_HEADS, GQA, n_rows, Q_BLOCK).transpose(0, 2, 1, 3)
            .reshape(N_KV_HEADS, n_rows, GQA * Q_BLOCK))


def _dot_tn(a, b):
    return lax.dot_general(a, b, (((0,), (0,)), ((), ())), preferred_element_type=F32)


def _nsa_kernel(qt_ref, kcp_ref, vcp_ref, vct_ref, ov_ref, ovt_ref, ka_ref, vst_ref, kw_ref, vwt_ref, gate_ref,
                b31_ref, b31p_ref, bcn_ref, bw_ref, bsn_ref, o_ref, qa_scr, m_scr, acc_scr, ow_scr, *, dh):
    qb = pl.program_id(2)
    q0 = qb * Q_BLOCK
    cols = GQA * Q_BLOCK
    qt = qt_ref[0, 0, 0]
    b31 = b31_ref[0]
    lane_q = lax.broadcasted_iota(jnp.int32, (1, cols), 1) % Q_BLOCK

    n_cp = kcp_ref.shape[2]
    cpb = Q_BLOCK // CMP_STRIDE
    s_far = _dot(kcp_ref[0, 0], qt) + b31
    row = lax.broadcasted_iota(jnp.int32, (n_cp, cols), 0)
    s_far = jnp.where((row >= CMP_PAD) & (row < cpb * qb), s_far, NEG_INF)
    near0 = pl.multiple_of(cpb * qb, SUBLANES)
    s_near = _dot(kcp_ref[0, 0, pl.ds(near0, CMP_NEAR), :], qt) + bcn_ref[0]
    rown = lax.broadcasted_iota(jnp.int32, (CMP_NEAR, cols), 0)
    s_near = jnp.where(rown + cpb * qb >= CMP_PAD, s_near, NEG_INF)
    m_c = jnp.maximum(jnp.max(s_far, axis=0, keepdims=True), jnp.max(s_near, axis=0, keepdims=True))
    p_far = jnp.exp(s_far - m_c)
    p_near = jnp.exp(s_near - m_c)
    l_c = jnp.sum(p_far, axis=0, keepdims=True) + jnp.sum(p_near, axis=0, keepdims=True)
    any_valid = (q0 + lane_q >= CMP_LEN - 1).astype(F32)
    scale_c = any_valid / l_c
    pf_hi = p_far.astype(BF16)
    pn_hi = p_near.astype(BF16)
    vc_near = vcp_ref[0, 0, pl.ds(near0, CMP_NEAR), :]
    o_c = (_dot(vct_ref[0, 0], pf_hi) + _dot_tn(vc_near, pn_hi)) * scale_c
    pf_lo = (p_far - pf_hi.astype(F32)).astype(BF16)
    pn_lo = (p_near - pn_hi.astype(F32)).astype(BF16)
    ov_near = ov_ref[pl.ds(near0, CMP_NEAR), :]
    imp4 = (_dot(ovt_ref[...], pf_hi) + _dot(ovt_ref[...], pf_lo)
            + _dot_tn(ov_near, pn_hi) + _dot_tn(ov_near, pn_lo)) * scale_c
    imp = imp4[:, 0:Q_BLOCK]
    for g in range(1, GQA):
        imp = imp + imp4[:, g * Q_BLOCK:(g + 1) * Q_BLOCK]

    n_win = WINDOW + Q_BLOCK
    ks0 = pl.multiple_of(q0, Q_BLOCK)
    s_w = _dot(kw_ref[0, 0, pl.ds(ks0, n_win), :], qt) + bw_ref[0]
    roww = lax.broadcasted_iota(jnp.int32, (n_win, cols), 0)
    s_w = jnp.where(roww + q0 >= WINDOW, s_w, NEG_INF)
    p_w = jnp.exp(s_w - jnp.max(s_w, axis=0, keepdims=True)).astype(BF16)
    acc_w = _dot(vwt_ref[0, 0, :, pl.ds(ks0, n_win)], p_w)
    gates = gate_ref[0, 0, 0]
    ow_scr[...] = gates[0:1, :] * o_c + gates[2:3, :] * (acc_w[0:dh, :] / acc_w[dh:dh + 1, :])

    n_sb = imp.shape[0]
    blk = lax.broadcasted_iota(jnp.int32, (n_sb, Q_BLOCK), 0)
    qq = lax.broadcasted_iota(jnp.int32, (n_sb, Q_BLOCK), 1)
    cur = (q0 + qq) // SEL_BLOCK
    forced = (blk == 0) | (blk == cur) | (blk == cur - 1)
    v = jnp.where(forced, FORCE_SCORE, jnp.where(blk <= cur, imp, -1.0))
    sel = jnp.zeros((n_sb, Q_BLOCK), F32)
    for _ in range(min(SEL_TOPN, n_sb)):
        mx = jnp.max(v, axis=0, keepdims=True)
        idx = jnp.min(jnp.where(v == mx, blk, n_sb), axis=0, keepdims=True)
        hit = blk == idx
        sel = jnp.where(hit, 1.0, sel)
        v = jnp.where(hit, -jnp.inf, v)
    near_blk0 = 2 * qb - SEL_NEAR // SEL_BLOCK // 2
    neg_near = jnp.where(sel > 0.5, 0.0, NEG_INF)
    neg_far = jnp.where(blk < near_blk0, neg_near, NEG_INF).astype(BF16)
    neg_near = neg_near.astype(BF16)

    qa_scr[n_sb:n_sb + dh, :] = qt
    qa_scr[n_sb + dh:, :] = jnp.zeros((KA_WIDTH - n_sb - dh, cols), BF16)
    for g in range(GQA):
        qa_scr[:n_sb, g * Q_BLOCK:(g + 1) * Q_BLOCK] = neg_near
    s_n =_dot(ka_ref[0, 0, pl.ds(ks0, SEL_NEAR), :], qa_scr[...]) + bsn_ref[0]
    rowk = lax.broadcasted_iota(jnp.int32, (SEL_NEAR, cols), 0)
    s_n = jnp.where(rowk + q0 >= SEL_NEAR // 2, s_n, NEG_INF)
    m0 = jnp.max(s_n, axis=0, keepdims=True)
    m_scr[...] = m0
    acc_scr[...] = _dot(vst_ref[0, 0, :, pl.ds(ks0, SEL_NEAR)], jnp.exp(s_n - m0).astype(BF16))

    for g in range(GQA):
        qa_scr[:n_sb, g * Q_BLOCK:(g + 1) * Q_BLOCK] = neg_far
    qa_scr[n_sb + dh:n_sb + dh + BIAS_ROWS, :] = b31p_ref[0]
    n_far = jnp.maximum(q0 - SEL_NEAR // 2 + SEL_FAR_KEYS - 1, 0) // SEL_FAR_KEYS

    def far_scores(k0):
        return _dot(ka_ref[0, 0, pl.ds(k0, SEL_FAR_KEYS), :], qa_scr[...])

    def far_update(k0, s_f):
        m_old = m_scr[...]
        m_new = jnp.maximum(m_old, jnp.max(s_f, axis=0, keepdims=True))
        p_f = jnp.exp(s_f - m_new).astype(BF16)
        acc_scr[...] = (jnp.exp(m_old - m_new) * acc_scr[...]
                        + _dot(vst_ref[0, 0, :, pl.ds(k0, SEL_FAR_KEYS)], p_f))
        m_scr[...] = m_new

    def far_pair(t, carry):
        k0 = pl.multiple_of(SEL_NEAR // 2 + 2 * t * SEL_FAR_KEYS, SEL_NEAR // 2)
        k1 = pl.multiple_of(k0 + SEL_FAR_KEYS, SEL_NEAR // 2)
        s_a = far_scores(k0)
        s_b = far_scores(k1)
        far_update(k0, s_a)
        far_update(k1, s_b)
        return carry

    lax.fori_loop(0, (n_far + 1) // 2, far_pair, 0)
    o_s = acc_scr[0:dh, :] / acc_scr[dh:dh + 1, :]

    o_ref[0, 0, 0] = (ow_scr[...] + gates[1:2, :] * o_s).astype(o_ref.dtype)


def _nsa(q_t, kc_pad, vc_pad, vc_t, ov_pad, ov_t, ka_pad, vs_t, kw_pad, vw_t, gates_t, b31, b31p, bcn, bw, bsn):
    B, HKV, NQB, dh, cols = q_t.shape
    per_bh = lambda b, h, i: (b, h, 0, 0)
    per_h = lambda b, h, i: (h, 0, 0)
    whole = lambda b, h, i: (0, 0)
    per_blk = lambda b, h, i: (b, h, i, 0, 0)
    bh_spec = lambda a: pl.BlockSpec((1, 1) + a.shape[2:], per_bh)
    h_spec = lambda a: pl.BlockSpec((1,) + a.shape[1:], per_h)
    return pl.pallas_call(
        functools.partial(_nsa_kernel, dh=dh),
        grid=(B, HKV, NQB),
        in_specs=[pl.BlockSpec((1, 1, 1, dh, cols), per_blk),
                  bh_spec(kc_pad), bh_spec(vc_pad), bh_spec(vc_t),
                  pl.BlockSpec(ov_pad.shape, whole), pl.BlockSpec(ov_t.shape, whole),
                  bh_spec(ka_pad), bh_spec(vs_t), bh_spec(kw_pad), bh_spec(vw_t),
                  pl.BlockSpec((1, 1, 1) + gates_t.shape[3:], per_blk),
                  h_spec(b31), h_spec(b31p), h_spec(bcn), h_spec(bw), h_spec(bsn)],
        out_specs=pl.BlockSpec((1, 1, 1, dh, cols), per_blk),
        out_shape=jax.ShapeDtypeStruct((B, HKV, NQB, dh, cols), BF16),
        scratch_shapes=[pltpu.VMEM((KA_WIDTH, cols), BF16), pltpu.VMEM((1, cols), F32),
                        pltpu.VMEM((vs_t.shape[2], cols), F32), pltpu.VMEM((dh, cols), F32)],
        compiler_params=_cparams("arbitrary", "arbitrary", "arbitrary"),
        name="nsa_attention",
    )(q_t, kc_pad, vc_pad, vc_t, ov_pad, ov_t, ka_pad, vs_t, kw_pad, vw_t, gates_t, b31, b31p, bcn, bw, bsn)


def _attention(att, gates, k_cmp, v_cmp, rel_bias, B, S, dh):
    HKV = N_KV_HEADS
    H = N_HEADS
    nqb = S // Q_BLOCK
    n_sb = S // SEL_BLOCK
    assert n_sb == LANES, "selection blocks are laid out on one lane tile"
    kvw = HKV * dh
    c0 = H * dh

    def kv_heads(j):
        a = att[:, c0 + j * kvw:c0 + (j + 1) * kvw].reshape(B, S, HKV, dh)
        return a.transpose(0, 2, 1, 3)

    def values_t(v, n_front, n_back):
        vt = jnp.concatenate([v.transpose(0, 1, 3, 2), jnp.ones((B, HKV, 1, S), BF16),
                              jnp.zeros((B, HKV, V_ROWS - dh - 1, S), BF16)], axis=2)
        return jnp.pad(vt, ((0, 0), (0, 0), (0, 0), (n_front, n_back)))

    q_t = (att[:, :c0].reshape(B, nqb, Q_BLOCK, HKV, GQA, dh).transpose(0, 3, 1, 5, 4, 2)
           .reshape(B, HKV, nqb, dh, GQA * Q_BLOCK))
    ks, vs, kw, vw = kv_heads(2), kv_heads(3), kv_heads(4), kv_heads(5)
    front = lambda a, n: jnp.pad(a, ((0, 0), (0, 0), (n, 0), (0, 0)))
    feat = np.zeros((S + SEL_FAR_KEYS, n_sb + BIAS_ROWS), np.float32)
    feat[np.arange(S), np.arange(S) // SEL_BLOCK] = 1.0
    feat[S:, n_sb - 1] = 1.0
    feat[:S, n_sb:n_sb + BIAS_PIECES] = 1.0
    feat = jnp.asarray(feat, BF16)
    ks_back = jnp.pad(ks, ((0, 0), (0, 0), (0, SEL_FAR_KEYS), (0, 0)))
    rows_k = S + SEL_FAR_KEYS
    ka = jnp.concatenate([jnp.broadcast_to(feat[:, :n_sb], (B, HKV, rows_k, n_sb)), ks_back,
                          jnp.broadcast_to(feat[:, n_sb:], (B, HKV, rows_k, BIAS_ROWS)),
                          jnp.zeros((B, HKV, rows_k, KA_WIDTH - n_sb - dh - BIAS_ROWS), BF16)], axis=-1)
    ka_pad = front(ka, SEL_NEAR // 2)
    vs_t = values_t(vs, SEL_NEAR // 2, SEL_FAR_KEYS)
    kw_pad = front(kw, WINDOW)
    vw_t = values_t(vw, WINDOW, 0)
    kc_pad = front(k_cmp, CMP_PAD)
    vc_pad = front(v_cmp, CMP_PAD)
    vc_t = vc_pad.transpose(0, 1, 3, 2)
    n_c = S // CMP_STRIDE
    c_start = np.arange(n_c)[:, None] * CMP_STRIDE
    s_start = np.arange(n_sb)[None, :] * SEL_BLOCK
    overlap = ((c_start < s_start + SEL_BLOCK) & (c_start + CMP_LEN > s_start)).astype(np.float32)
    overlap[n_c - 1] = 0.0
    overlap = np.concatenate([np.zeros((CMP_PAD, n_sb), np.float32), overlap])
    ov_pad = jnp.asarray(overlap, BF16)
    ov_t = jnp.asarray(overlap.T.copy(), BF16)

    gates_t = (gates[:, :3 * H].reshape(B, nqb, Q_BLOCK, HKV, GQA, 3).transpose(0, 3, 1, 5, 4, 2)
               .reshape(B, HKV, nqb, 3, GQA * Q_BLOCK))
    gates_t = jnp.pad(gates_t, ((0, 0), (0, 0), (0, 0), (0, SUBLANES - 3), (0, 0)))

    far = S + WINDOW
    bcn = _bias_table(rel_bias, CMP_STRIDE * CMP_PAD - (CMP_LEN - 1), CMP_NEAR, 0, far, row_step=CMP_STRIDE)
    bw = _bias_table(rel_bias, WINDOW, WINDOW + Q_BLOCK, 0, WINDOW)
    bsn = _bias_table(rel_bias, SEL_NEAR // 2, SEL_NEAR, 0, far)
    b31 = jnp.repeat(rel_bias.astype(F32)[NUM_BUCKETS - 1].reshape(HKV, GQA), Q_BLOCK, axis=1)[:, None, :]
    pieces, rest = [], b31
    for _ in range(BIAS_PIECES):
        piece = rest.astype(BF16)
        pieces.append(piece)
        rest = rest - piece.astype(F32)
    b31p = jnp.concatenate(pieces + [jnp.zeros((HKV, BIAS_ROWS - BIAS_PIECES, GQA * Q_BLOCK), BF16)], axis=1)

    o = _nsa(q_t, kc_pad, vc_pad, vc_t, ov_pad, ov_t, ka_pad, vs_t, kw_pad, vw_t, gates_t, b31, b31p, bcn, bw, bsn)
    o = o.reshape(B, HKV, nqb, dh, GQA, Q_BLOCK).transpose(0, 2, 5, 1, 4, 3)
    return o.reshape(B * S, H * dh)


def _outproj_kernel(h_ref, yr_ref, ya_ref, wr_ref, wa_ref, g_ref, b_ref, o_ref):
    mix = _dot(yr_ref[...], wr_ref[...]) + _dot(ya_ref[...], wa_ref[...])
    o_ref[...] = _layer_norm(ALPHA * h_ref[...] + mix, g_ref[...], b_ref[...])


def _outproj_ln(h, y_rnn, y_att, w_r, w_a, g, b):
    T, D = h.shape
    dr, da = y_rnn.shape[1], y_att.shape[1]
    tm = PROJ_ROWS
    row = lambda i: (i, 0)
    full = lambda i: (0, 0)
    return pl.pallas_call(
        _outproj_kernel,
        grid=(T // tm,),
        in_specs=[pl.BlockSpec((tm, D), row), pl.BlockSpec((tm, dr), row), pl.BlockSpec((tm, da), row),
                  pl.BlockSpec((dr, D), full), pl.BlockSpec((da, D), full),
                  pl.BlockSpec((1, D), full), pl.BlockSpec((1, D), full)],
        out_specs=pl.BlockSpec((tm, D), row),
        out_shape=jax.ShapeDtypeStruct((T, D), F32),
        compiler_params=_cparams("arbitrary"),
        name="outproj_ln",
    )(h, y_rnn, y_att, w_r, w_a, g, b)


def _route_kernel(h_ref, wh_ref, wl_ref, rb_ref, tri_ref, e_ref, p_ref, w_ref, cnt_ref, carry_scr, *, cols):
    i = pl.program_id(0)

    @pl.when(i == 0)
    def _():
        carry_scr[...] = jnp.zeros_like(carry_scr)

    h = h_ref[...]
    h_hi = h.astype(BF16)
    h_lo = (h - h_hi.astype(F32)).astype(BF16)
    logits = _dot_nt(wh_ref[...], h_hi) + (_dot_nt(wl_ref[...], h_hi) + _dot_nt(wh_ref[...], h_lo))
    scores = _sigmoid(logits)
    biased = scores + rb_ref[...]
    gsz = N_EXPERTS // N_GROUPS
    gscore = []
    for g in range(N_GROUPS):
        blk = biased[g * gsz:(g + 1) * gsz]
        m1 = jnp.max(blk, axis=0, keepdims=True)
        is_m1 = blk == m1
        n_m1 = jnp.sum(jnp.where(is_m1, 1.0, 0.0), axis=0, keepdims=True)
        m2 = jnp.max(jnp.where(is_m1, -jnp.inf, blk), axis=0, keepdims=True)
        gscore.append(m1 + jnp.where(n_m1 > 1.5, m1, m2))
    masked = []
    for g in range(N_GROUPS):
        rank = jnp.zeros_like(gscore[g])
        for o in range(N_GROUPS):
            if o == g:
                continue
            ahead = (gscore[o] > gscore[g]) | (gscore[o] == gscore[g]) if o < g else gscore[o] > gscore[g]
            rank = rank + jnp.where(ahead, 1.0, 0.0)
        keep = rank < TOPK_GROUPS - 0.5
        masked.append(jnp.where(keep, biased[g * gsz:(g + 1) * gsz], NEG_INF))
    v = jnp.concatenate(masked, axis=0)
    rows = lax.broadcasted_iota(jnp.int32, (N_EXPERTS, cols), 0)
    sel = jnp.zeros((N_EXPERTS, cols), F32)
    idxs, svals = [], []
    for _ in range(TOP_K):
        mx = jnp.max(v, axis=0, keepdims=True)
        idx = jnp.min(jnp.where(v == mx, rows, N_EXPERTS), axis=0, keepdims=True)
        hit = rows == idx
        idxs.append(idx)
        svals.append(jnp.sum(jnp.where(hit, scores, 0.0), axis=0, keepdims=True))
        sel = jnp.where(hit, 1.0, sel)
        v = jnp.where(hit, -jnp.inf, v)
    total = svals[0]
    for k in range(1, TOP_K):
        total = total + svals[k]
    prefix = _dot(sel.astype(BF16), tri_ref[...]) + carry_scr[:, 0:1]
    for k in range(TOP_K):
        hit = rows == idxs[k]
        e_ref[k:k + 1, :] = idxs[k]
        p_ref[k:k + 1, :] = jnp.sum(jnp.where(hit, prefix, 0.0), axis=0, keepdims=True).astype(jnp.int32)
        w_ref[k:k + 1, :] = svals[k] / total * ROUTED_SCALE
    new_carry = carry_scr[...] + jnp.sum(sel, axis=1, keepdims=True)
    carry_scr[...] = new_carry
    cnt_ref[...] = new_carry


def _route(h1, rw_hi, rw_lo, rb, tri):
    T, D = h1.shape
    cols = ROUTE_COLS
    full = lambda i: (0, 0)
    out_col = pl.BlockSpec((TOP_K, cols), lambda i: (0, i))
    return pl.pallas_call(
        functools.partial(_route_kernel, cols=cols),
        grid=(T // cols,),
        in_specs=[pl.BlockSpec((cols, D), lambda i: (i, 0)),
                  pl.BlockSpec((N_EXPERTS, D), full), pl.BlockSpec((N_EXPERTS, D), full),
                  pl.BlockSpec((N_EXPERTS, 1), full), pl.BlockSpec((cols, cols), full)],
        out_specs=[out_col, out_col, out_col, pl.BlockSpec((N_EXPERTS, LANES), full)],
        out_shape=[jax.ShapeDtypeStruct((TOP_K, T), jnp.int32), jax.ShapeDtypeStruct((TOP_K, T), jnp.int32),
                   jax.ShapeDtypeStruct((TOP_K, T), F32), jax.ShapeDtypeStruct((N_EXPERTS, LANES), F32)],
        scratch_shapes=[pltpu.VMEM((N_EXPERTS, LANES), F32)],
        compiler_params=_cparams("arbitrary"),
        name="router",
    )(h1, rw_hi, rw_lo, rb, tri)


def _tile_major(a, tile):
    k, t = a.shape
    return a.reshape(k, t // tile, tile).transpose(1, 0, 2).reshape(-1)


def _load_indices(e_hbm, p_hbm, e_smem, p_smem, idx_sem, n):
    base = pl.multiple_of(pl.program_id(0) * n, n)
    ce = pltpu.make_async_copy(e_hbm.at[pl.ds(base, n)], e_smem, idx_sem.at[0])
    cp = pltpu.make_async_copy(p_hbm.at[pl.ds(base, n)], p_smem, idx_sem.at[1])
    ce.start()
    cp.start()
    ce.wait()
    cp.wait()


def _issue_rows(n, make_copy):
    def body(c, carry):
        for u in range(ISSUE_UNROLL):
            make_copy(c * ISSUE_UNROLL + u).start(priority=u % 2)
        return carry

    lax.fori_loop(0, n // ISSUE_UNROLL, body, 0)


def _token_tile(ref, t):
    return ref.at[pl.ds(pl.multiple_of(t * SUBLANES, SUBLANES), SUBLANES), :]


def _to_token_tiles(dst_ref, x, n):
    for c in range(SUBLANES):
        dst_ref[pl.ds(c, n, stride=SUBLANES), :] = x[:, c * LANES:(c + 1) * LANES]


def _from_token_tiles(src_ref, first, n, c):
    return src_ref[pl.ds(first * SUBLANES + c, n, stride=SUBLANES), :]


def _dispatch_kernel(pstart_ref, h_ref, e_hbm, p_hbm, xs_out, e_smem, p_smem, stage, idx_sem, row_sem, *, rows):
    n = TOP_K * rows
    _load_indices(e_hbm, p_hbm, e_smem, p_smem, idx_sem, n)
    _to_token_tiles(stage, h_ref[...], rows)

    def row_copy(j):
        dst = pstart_ref[e_smem[j]] + p_smem[j]
        return pltpu.make_async_copy(_token_tile(stage, j & (rows - 1)), _token_tile(xs_out, dst), row_sem)

    _issue_rows(n, row_copy)
    whole = xs_out.at[pl.ds(0, n * SUBLANES), :]
    pltpu.make_async_copy(whole, whole, row_sem).wait()


def _dispatch(h1, e_flat, p_flat, pstart, n_rows):
    T, D = h1.shape
    rows = DISPATCH_ROWS
    n = TOP_K * rows
    return pl.pallas_call(
        functools.partial(_dispatch_kernel, rows=rows),
        grid_spec=pltpu.PrefetchScalarGridSpec(
            num_scalar_prefetch=1,
            grid=(T // rows,),
            in_specs=[pl.BlockSpec((rows, D), lambda i, ps: (i, 0)),
                      pl.BlockSpec(memory_space=pl.ANY), pl.BlockSpec(memory_space=pl.ANY)],
            out_specs=pl.BlockSpec(memory_space=pl.ANY),
            scratch_shapes=[pltpu.SMEM((n,), jnp.int32), pltpu.SMEM((n,), jnp.int32),
                            pltpu.VMEM((rows * SUBLANES, LANES), F32),
                            pltpu.SemaphoreType.DMA((2,)), pltpu.SemaphoreType.DMA(())]),
        out_shape=jax.ShapeDtypeStruct((n_rows * SUBLANES, LANES), F32),
        compiler_params=_cparams("arbitrary"),
        name="moe_dispatch",
    )(pstart, h1, e_flat, p_flat)


def _expert_kernel(be_ref, nv_ref, na_ref, x_ref, wg_ref, wu_ref, wd_ref, y_ref, xb, wgb, wub, wdb):
    del na_ref
    i = pl.program_id(0)
    bm = x_ref.shape[0] // SUBLANES

    @pl.when((i == 0) | (be_ref[i] != be_ref[jnp.maximum(i - 1, 0)]))
    def _():
        wgb[...] = wg_ref[0].astype(BF16)
        wub[...] = wu_ref[0].astype(BF16)
        wdb[...] = wd_ref[0].astype(BF16)

    @pl.when(nv_ref[i] > 0)
    def _():
        valid = lax.broadcasted_iota(jnp.int32, (bm, LANES), 0) < nv_ref[i]
        for c in range(SUBLANES):
            x_c = _from_token_tiles(x_ref, 0, bm, c)
            xb[:, c * LANES:(c + 1) * LANES] = jnp.where(valid, x_c, 0.0).astype(BF16)
        x = xb[...]
        g = _dot(x, wgb[...])
        u = _dot(x, wub[...])
        hmid = (g * _sigmoid(g)) * u
        _to_token_tiles(y_ref, _dot(hmid.astype(BF16), wdb[...]), bm)

    @pl.when(nv_ref[i] <= 0)
    def _():
        y_ref[...] = jnp.zeros_like(y_ref)


def _experts(xs, w_gate, w_up, w_down, block_e, n_valid, n_active):
    R = xs.shape[0] // SUBLANES
    E, D, de = w_gate.shape
    bm = EXPERT_ROWS
    x_map = lambda i, be, nv, na: (jnp.minimum(i, jnp.maximum(na[0] - 1, 0)), 0)
    w_map = lambda i, be, nv, na: (be[i], 0, 0)
    return pl.pallas_call(
        _expert_kernel,
        grid_spec=pltpu.PrefetchScalarGridSpec(
            num_scalar_prefetch=3,
            grid=(R // bm,),
            in_specs=[pl.BlockSpec((bm * SUBLANES, LANES), x_map), pl.BlockSpec((1, D, de), w_map),
                      pl.BlockSpec((1, D, de), w_map), pl.BlockSpec((1, de, D), w_map)],
            out_specs=pl.BlockSpec((bm * SUBLANES, LANES), lambda i, be, nv, na: (i, 0)),
            scratch_shapes=[pltpu.VMEM((bm, D), BF16), pltpu.VMEM((D, de), BF16),
                            pltpu.VMEM((D, de), BF16), pltpu.VMEM((de, D), BF16)]),
        out_shape=jax.ShapeDtypeStruct((R * SUBLANES, LANES), F32),
        compiler_params=_cparams("arbitrary"),
        name="moe_experts",
    )(block_e, n_valid, n_active, xs, w_gate, w_up, w_down)


def _combine_kernel(pstart_ref, h_ref, wk_ref, e_hbm, p_hbm, ys_hbm, sg_ref, su_ref, sd_ref, g_ref, b_ref,
                    o_ref, e_smem, p_smem, buf, idx_sem, row_sem, *, rows):
    n = TOP_K * rows
    _load_indices(e_hbm, p_hbm, e_smem, p_smem, idx_sem, n)

    def row_copy(j):
        src = pstart_ref[e_smem[j]] + p_smem[j]
        return pltpu.make_async_copy(_token_tile(ys_hbm, src), _token_tile(buf, j), row_sem)

    _issue_rows(n, row_copy)
    h = h_ref[...]
    hb = h.astype(BF16)
    g = _dot(hb, sg_ref[...])
    u = _dot(hb, su_ref[...])
    shared = _dot(((g * _sigmoid(g)) * u).astype(BF16), sd_ref[...])
    pltpu.make_async_copy(ys_hbm.at[pl.ds(0, n * SUBLANES), :], buf, row_sem).wait()
    wk = wk_ref[...]
    cols = []
    for c in range(SUBLANES):
        acc = wk[:, 0:1] * _from_token_tiles(buf, 0, rows, c)
        for k in range(1, TOP_K):
            acc = acc + wk[:, k:k + 1] * _from_token_tiles(buf, k * rows, rows, c)
        cols.append(acc)
    routed = jnp.concatenate(cols, axis=1)
    o_ref[...] = _layer_norm(ALPHA * h + (routed + shared), g_ref[...], b_ref[...])


def _combine(h1, wk_t, e_flat, p_flat, pstart, ys, sh_gate, sh_up, sh_down, g, b):
    T, D = h1.shape
    ds = sh_gate.shape[1]
    rows = COMBINE_ROWS
    n = TOP_K * rows
    full = lambda i, ps: (0, 0)
    return pl.pallas_call(
        functools.partial(_combine_kernel, rows=rows),
        grid_spec=pltpu.PrefetchScalarGridSpec(
            num_scalar_prefetch=1,
            grid=(T // rows,),
            in_specs=[pl.BlockSpec((rows, D), lambda i, ps: (i, 0)),
                      pl.BlockSpec((rows, TOP_K), lambda i, ps: (i, 0)),
                      pl.BlockSpec(memory_space=pl.ANY), pl.BlockSpec(memory_space=pl.ANY),
                      pl.BlockSpec(memory_space=pl.ANY),
                      pl.BlockSpec((D, ds), full), pl.BlockSpec((D, ds), full), pl.BlockSpec((ds, D), full),
                      pl.BlockSpec((1, D), full), pl.BlockSpec((1, D), full)],
            out_specs=pl.BlockSpec((rows, D), lambda i, ps: (i, 0)),
            scratch_shapes=[pltpu.SMEM((n,), jnp.int32), pltpu.SMEM((n,), jnp.int32),
                            pltpu.VMEM((n * SUBLANES, LANES), F32),
                            pltpu.SemaphoreType.DMA((2,)), pltpu.SemaphoreType.DMA(())]),
        out_shape=jax.ShapeDtypeStruct((T, D), F32),
        compiler_params=_cparams("arbitrary"),
        name="moe_combine",
    )(pstart, h1, wk_t, e_flat, p_flat, ys, sh_gate, sh_up, sh_down, g, b)


def _moe(h1, router_w, router_b, w_gate, w_up, w_down, sh_gate, sh_up, sh_down, ln_g, ln_b):
    T, D = h1.shape
    assert D == SUBLANES * LANES, "a token row is moved as one (SUBLANES, LANES) tile"
    rw_t = router_w.T
    rw_hi = rw_t.astype(BF16)
    rw_lo = (rw_t - rw_hi.astype(F32)).astype(BF16)
    tri = jnp.asarray(np.triu(np.ones((ROUTE_COLS, ROUTE_COLS), np.float32), 1), BF16)
    eidx, pos, wk, cnt = _route(h1, rw_hi, rw_lo, router_b.reshape(N_EXPERTS, 1), tri)

    bm = EXPERT_ROWS
    counts = cnt[:, 0].astype(jnp.int32)
    padded = (counts + bm - 1) // bm * bm
    pend = jnp.cumsum(padded)
    pstart = (pend - padded).astype(jnp.int32)
    n_blocks = T * TOP_K // bm + N_EXPERTS
    n_active = (pend[-1:] // bm).astype(jnp.int32)
    block_e = jnp.clip(jnp.searchsorted(pend, jnp.arange(n_blocks, dtype=jnp.int32) * bm, side='right'),
                       0, N_EXPERTS - 1).astype(jnp.int32)

    n_valid = jnp.clip(counts[block_e] - (jnp.arange(n_blocks, dtype=jnp.int32) * bm - pstart[block_e]), 0, bm)
    xs = _dispatch(h1, _tile_major(eidx, DISPATCH_ROWS), _tile_major(pos, DISPATCH_ROWS), pstart, n_blocks * bm)
    ys = _experts(xs, w_gate, w_up, w_down, block_e, n_valid.astype(jnp.int32), n_active)
    return _combine(h1, wk.T, _tile_major(eidx, COMBINE_ROWS), _tile_major(pos, COMBINE_ROWS), pstart, ys,
                    sh_gate.astype(BF16), sh_up.astype(BF16), sh_down.astype(BF16),
                    ln_g.reshape(1, D), ln_b.reshape(1, D))


def _block_diag(w):
    n, c, d = w.shape
    eye = jnp.eye(n, dtype=w.dtype)
    return (eye[:, None, :, None] * w[:, :, None, :]).reshape(n * c, n * d)


def _mixer_and_ln(h_in_x, ln_g, ln_b, w_in, conv_w, conv_b, rg_w_a, rg_b_a, rg_w_x, rg_b_x, rg_lambda,
                  cmp_k, cmp_v, rel_bias, w_out, ln1_g, ln1_b, B, S, first):
    T, D = h_in_x.shape
    d_rnn = D // D_RNN_FRAC
    dh = (D - d_rnn) // N_HEADS
    kvw = N_KV_HEADS * dh
    n_att = N_HEADS * dh + 6 * kvw
    c_rg, c_att = 2 * d_rnn, 2 * d_rnn + n_att
    w_rg = w_in[:, :c_rg].astype(BF16)
    w_att = w_in[:, c_rg:c_att].astype(BF16)
    w_gl = jnp.pad(w_in[:, c_att:], ((0, 0), (0, LANES - 3 * N_HEADS))).astype(BF16)
    if first:
        g0, b0 = ln_g.reshape(1, D), ln_b.reshape(1, D)
    else:
        raise NotImplementedError("depth > 1")
    h, rg, att, gates = _ln_inproj(h_in_x, g0, b0, w_rg, w_att, w_gl, q_cols=N_HEADS * dh, q_scale=dh ** -0.5)

    y_rnn = _rglru(rg.reshape(B, S, c_rg), conv_w, conv_b.reshape(1, d_rnn),
                   _block_diag(rg_w_a).astype(BF16), rg_b_a.reshape(1, d_rnn),
                   _block_diag(rg_w_x).astype(BF16), rg_b_x.reshape(1, d_rnn), rg_lambda.reshape(1, d_rnn))

    nch = S // CMP_STRIDE
    c0 = N_HEADS * dh

    def chunks_of(j):
        a = att[:, c0 + j * kvw:c0 + (j + 1) * kvw].reshape(B, nch, CMP_STRIDE, N_KV_HEADS, dh)
        return a.transpose(0, 3, 1, 2, 4).reshape(B, N_KV_HEADS, nch, CMP_STRIDE * dh)

    chunks = jnp.stack([chunks_of(0), chunks_of(1)])
    half = CMP_STRIDE * dh
    stack = lambda i: jnp.stack([cmp_k[i], cmp_v[i]])
    pos = jnp.broadcast_to(stack(0).reshape(2, 1, CMP_LEN * dh), (2, SUBLANES, CMP_LEN * dh)).astype(BF16)
    w1 = stack(1).astype(BF16)
    kv_cmp = _compress(chunks, pos, w1[:, :half], w1[:, half:], stack(2)[:, None, :],
                       stack(3).astype(BF16), stack(4)[:, None, :])

    y_att = _attention(att, gates, kv_cmp[0], kv_cmp[1], rel_bias, B, S, dh)
    w_o = w_out.astype(BF16)
    return _outproj_ln(h, y_rnn.reshape(T, d_rnn), y_att, w_o[:d_rnn], w_o[d_rnn:],
                       ln1_g.reshape(1, D), ln1_b.reshape(1, D))


def kernel(x, ln_in_g, ln_in_b, w_in, conv_w, conv_b, rg_w_a, rg_b_a, rg_w_x, rg_b_x, rg_lambda, cmp_pos_k, cmp_k_w1, cmp_k_b1, cmp_k_w2, cmp_k_b2, cmp_pos_v, cmp_v_w1, cmp_v_b1, cmp_v_w2, cmp_v_b2, rel_bias, w_out, ln1_g, ln1_b, router_w, router_b, w_gate, w_up, w_down, sh_gate, sh_up, sh_down, ln2_g, ln2_b):
    B, S, D = x.shape
    assert w_in.shape[0] == DEPTH
    l = 0
    cmp_k = (cmp_pos_k[l], cmp_k_w1[l], cmp_k_b1[l], cmp_k_w2[l], cmp_k_b2[l])
    cmp_v = (cmp_pos_v[l], cmp_v_w1[l], cmp_v_b1[l], cmp_v_w2[l], cmp_v_b2[l])
    h1 = _mixer_and_ln(x.reshape(B * S, D), ln_in_g, ln_in_b, w_in[l], conv_w[l], conv_b[l], rg_w_a[l], rg_b_a[l],
                       rg_w_x[l], rg_b_x[l], rg_lambda[l], cmp_k, cmp_v, rel_bias, w_out[l], ln1_g[l], ln1_b[l],
                       B, S, True)
    out = _moe(h1, router_w[l], router_b[l], w_gate[l], w_up[l], w_down[l], sh_gate[l], sh_up[l], sh_down[l],
               ln2_g[l], ln2_b[l])
    return out.reshape(B, S, D)
```

```python
import functools
import math

import numpy as np
import jax
import jax.numpy as jnp
from jax import lax
from jax.experimental import pallas as pl
from jax.experimental.pallas import tpu as pltpu

F32 = jnp.float32
BF16 = jnp.bfloat16

DEPTH = 1
D_RNN_FRAC = 2
RG_BLOCKS = 8
CONV_W = 4
RG_C = 8.0
N_HEADS = 8
N_KV_HEADS = 2
GQA = N_HEADS // N_KV_HEADS
CMP_STRIDE = 16
CMP_LEN = 2 * CMP_STRIDE
SEL_BLOCK = 64
SEL_TOPN = 16
WINDOW = 512
Q_BLOCK = 128
NUM_BUCKETS = 32
MAX_DISTANCE = 128
N_EXPERTS = 256
TOP_K = 8
N_GROUPS = 8
TOPK_GROUPS = 4
ROUTED_SCALE = 2.5
ALPHA = (2 * DEPTH) ** 0.25
LN_EPS = 1e-5
NEG_INF = -1e30
FORCE_SCORE = 1e9

LANES = 128
SUBLANES = 8
VMEM_LIMIT_BYTES = 56 * 1024 * 1024

PROJ_ROWS = 512
RG_ROWS = 256
ROUTE_COLS = 512
DISPATCH_ROWS = 256
EXPERT_ROWS = 256
COMBINE_ROWS = 128
ISSUE_UNROLL = 8
SEL_FAR_KEYS = 512
CMP_NEAR = 32
CMP_PAD = CMP_NEAR - Q_BLOCK // CMP_STRIDE
SEL_NEAR = 2 * Q_BLOCK
KA_WIDTH = 256
BIAS_PIECES = 3
BIAS_ROWS = 16
V_ROWS = 80


def _cparams(*sem):
    return pltpu.CompilerParams(dimension_semantics=sem, vmem_limit_bytes=VMEM_LIMIT_BYTES)


def _sigmoid(x):
    return 1.0 / (1.0 + jnp.exp(-x))


def _gelu_tanh(x):
    return 0.5 * x * (1.0 + jnp.tanh(math.sqrt(2.0 / math.pi) * (x + 0.044715 * (x * x * x))))


def _layer_norm(x, g, b):
    mu = jnp.mean(x, axis=-1, keepdims=True)
    xc = x - mu
    var = jnp.mean(xc * xc, axis=-1, keepdims=True)
    return xc * lax.rsqrt(var + LN_EPS) * g + b


def _dot(a, b):
    return jnp.dot(a, b, preferred_element_type=F32)


def _dot_nt(a, b):
    return lax.dot_general(a, b, (((1,), (1,)), ((), ())), preferred_element_type=F32)


def _ln_inproj_kernel(x_ref, g_ref, b_ref, wrg_ref, watt_ref, wgl_ref,
                      h_ref, rg_ref, att_ref, gate_ref, *, q_cols, q_scale):
    h = _layer_norm(x_ref[...], g_ref[...], b_ref[...])
    h_ref[...] = h
    hb = h.astype(BF16)
    rg_ref[...] = _dot(hb, wrg_ref[...])
    att = _dot(hb, watt_ref[...])
    att_ref[:, :q_cols] = (att[:, :q_cols] * q_scale).astype(BF16)
    att_ref[:, q_cols:] = att[:, q_cols:].astype(BF16)
    gate_ref[...] = _sigmoid(_dot(hb, wgl_ref[...]))


def _ln_inproj(x2, g, b, w_rg, w_att, w_gl, *, q_cols, q_scale):
    T, D = x2.shape
    n_rg, n_att, n_gl = w_rg.shape[1], w_att.shape[1], w_gl.shape[1]
    tm = PROJ_ROWS
    row = lambda i: (i, 0)
    full = lambda i: (0, 0)
    return pl.pallas_call(
        functools.partial(_ln_inproj_kernel, q_cols=q_cols, q_scale=q_scale),
        grid=(T // tm,),
        in_specs=[pl.BlockSpec((tm, D), row), pl.BlockSpec((1, D), full), pl.BlockSpec((1, D), full),
                  pl.BlockSpec((D, n_rg), full), pl.BlockSpec((D, n_att), full), pl.BlockSpec((D, n_gl), full)],
        out_specs=[pl.BlockSpec((tm, D), row), pl.BlockSpec((tm, n_rg), row),
                   pl.BlockSpec((tm, n_att), row), pl.BlockSpec((tm, n_gl), row)],
        out_shape=[jax.ShapeDtypeStruct((T, D), F32), jax.ShapeDtypeStruct((T, n_rg), F32),
                   jax.ShapeDtypeStruct((T, n_att), BF16), jax.ShapeDtypeStruct((T, n_gl), F32)],
        compiler_params=_cparams("arbitrary"),
        name="ln_inproj",
    )(x2, g, b, w_rg, w_att, w_gl)


def _rglru_kernel(rg_ref, cw_ref, cb_ref, wa_ref, ba_ref, wx_ref, bx_ref, lam_ref, y_ref,
                  xprev_scr, hprev_scr, *, d_rnn, rows):
    s = pl.program_id(1)

    @pl.when(s == 0)
    def _():
        xprev_scr[...] = jnp.zeros_like(xprev_scr)
        hprev_scr[...] = jnp.zeros_like(hprev_scr)

    xr = rg_ref[0, :, :d_rnn]
    gr = rg_ref[0, :, d_rnn:]
    xcat = jnp.concatenate([xprev_scr[...], xr], axis=0)
    xc = cw_ref[CONV_W - 1:CONV_W, :] * xr + cb_ref[...]
    for k in range(1, CONV_W):
        shifted = pltpu.roll(xcat, k, 0)[SUBLANES:, :]
        xc = xc + cw_ref[CONV_W - 1 - k:CONV_W - k, :] * shifted
    xprev_scr[...] = xr[rows - SUBLANES:, :]

    xcb = xc.astype(BF16)
    r = _sigmoid(_dot(xcb, wa_ref[...]) + ba_ref[...])
    i = _sigmoid(_dot(xcb, wx_ref[...]) + bx_ref[...])
    neg_lam = -lam_ref[...]
    softplus = jnp.maximum(neg_lam, 0.0) + jnp.log1p(jnp.exp(-jnp.abs(neg_lam)))
    log_a = -RG_C * r * softplus
    a = jnp.exp(log_a)
    u = jnp.sqrt(jnp.tanh(-log_a) * (1.0 + a * a)) * (i * xc)

    t_idx = lax.broadcasted_iota(jnp.int32, (rows, d_rnn), 0)
    d = 1
    while d < rows:
        keep = t_idx >= d
        a_sh = jnp.where(keep, pltpu.roll(a, d, 0), 1.0)
        u_sh = jnp.where(keep, pltpu.roll(u, d, 0), 0.0)
        u = u + a * u_sh
        a = a * a_sh
        d *= 2
    h = u + a * hprev_scr[0:1, :]
    hprev_scr[...] = jnp.broadcast_to(h[rows - 1:rows, :], hprev_scr.shape)
    y_ref[0] = (h * _gelu_tanh(gr)).astype(y_ref.dtype)


def _rglru(rg3, conv_w, conv_b, wa_bd, b_a, wx_bd, b_x, lam):
    B, S, two_d = rg3.shape
    d = two_d // 2
    rows = RG_ROWS
    full = lambda b, s: (0, 0)
    return pl.pallas_call(
        functools.partial(_rglru_kernel, d_rnn=d, rows=rows),
        grid=(B, S // rows),
        in_specs=[pl.BlockSpec((1, rows, two_d), lambda b, s: (b, s, 0)),
                  pl.BlockSpec((CONV_W, d), full), pl.BlockSpec((1, d), full),
                  pl.BlockSpec((d, d), full), pl.BlockSpec((1, d), full),
                  pl.BlockSpec((d, d), full), pl.BlockSpec((1, d), full), pl.BlockSpec((1, d), full)],
        out_specs=pl.BlockSpec((1, rows, d), lambda b, s: (b, s, 0)),
        out_shape=jax.ShapeDtypeStruct((B, S, d), BF16),
        scratch_shapes=[pltpu.VMEM((SUBLANES, d), F32), pltpu.VMEM((SUBLANES, d), F32)],
        compiler_params=_cparams("arbitrary", "arbitrary"),
        name="rglru",
    )(rg3, conv_w, conv_b, wa_bd, b_a, wx_bd, b_x, lam)


def _compress_kernel(x_ref, pos_ref, w1a_ref, w1b_ref, b1_ref, w2_ref, b2_ref, o_ref, *, n_chunks):
    x = x_ref[0, 0, 0]
    w1a = w1a_ref[0]
    w1b = w1b_ref[0]
    pos = pos_ref[0]
    half = x.shape[1]
    pos_term = _dot(pos[:, :half], w1a) + _dot(pos[:, half:], w1b)
    first = _dot(x, w1a)
    second = _dot(x, w1b)
    hidden = first + pltpu.roll(second, n_chunks - 1, 0) + pos_term[0:1, :] + b1_ref[0]
    out = _dot(_gelu_tanh(hidden).astype(BF16), w2_ref[0]) + b2_ref[0]
    rows = lax.broadcasted_iota(jnp.int32, out.shape, 0)
    o_ref[0, 0, 0] = jnp.where(rows < n_chunks - 1, out, 0.0).astype(o_ref.dtype)


def _compress(chunks, pos, w1a, w1b, b1, w2, b2):
    two, B, HKV, NCH, half = chunks.shape
    hid = w1a.shape[-1]
    dh = w2.shape[-1]
    kv = lambda c, b, h: (c, 0, 0)
    return pl.pallas_call(
        functools.partial(_compress_kernel, n_chunks=NCH),
        grid=(two, B, HKV),
        in_specs=[pl.BlockSpec((1, 1, 1, NCH, half), lambda c, b, h: (c, b, h, 0, 0)),
                  pl.BlockSpec((1, SUBLANES, 2 * half), kv),
                  pl.BlockSpec((1, half, hid), kv), pl.BlockSpec((1, half, hid), kv),
                  pl.BlockSpec((1, 1, hid), kv), pl.BlockSpec((1, hid, dh), kv), pl.BlockSpec((1, 1, dh), kv)],
        out_specs=pl.BlockSpec((1, 1, 1, NCH, dh), lambda c, b, h: (c, b, h, 0, 0)),
        out_shape=jax.ShapeDtypeStruct((two, B, HKV, NCH, dh), BF16),
        compiler_params=_cparams("arbitrary", "arbitrary", "arbitrary"),
        name="compress_kv",
    )(chunks, pos, w1a, w1b, b1, w2, b2)


def _t5_bucket_np(dist):
    n = np.maximum(dist, 0)
    max_exact = NUM_BUCKETS // 2
    nf = np.maximum(n, 1).astype(np.float32)
    large = max_exact + (np.log(nf / np.float32(max_exact)) / np.float32(math.log(MAX_DISTANCE / max_exact))
                         * np.float32(NUM_BUCKETS - max_exact)).astype(np.int32)
    return np.where(n < max_exact, n, np.minimum(large, NUM_BUCKETS - 1)).astype(np.int32)


def _bias_table(rel_bias, offset, n_rows, lo, hi, row_step=1):
    R = n_rows * row_step
    P = R + Q_BLOCK - 1
    d = np.arange(P) + (offset - (R - 1))
    f = rel_bias.astype(F32)[jnp.asarray(_t5_bucket_np(d))]
    f = jnp.where(jnp.asarray((d >= lo) & (d < hi))[:, None], f, NEG_INF).T
    hankel = jnp.tile(f, (1, R + 1))[:, :R * (P + 1)].reshape(N_HEADS, R, P + 1)[:, :, :Q_BLOCK]
    tab = hankel[:, ::-1][:, ::row_step]
    return (tab.reshape(N_KV_HEADS, GQA, n_rows, Q_BLOCK).transpose(0, 2, 1, 3)
            .reshape(N_KV_HEADS, n_rows, GQA * Q_BLOCK))


def _dot_tn(a, b):
    return lax.dot_general(a, b, (((0,), (0,)), ((), ())), preferred_element_type=F32)


def _nsa_kernel(qt_ref, kcp_ref, vcp_ref, vct_ref, ov_ref, ovt_ref, ka_ref, vst_ref, kw_ref, vwt_ref, gate_ref,
                b31_ref, b31p_ref, bcn_ref, bw_ref, bsn_ref, o_ref, qa_scr, m_scr, acc_scr, ow_scr,
                sa_scr, sb_scr, *, dh):
    qb = pl.program_id(2)
    q0 = qb * Q_BLOCK
    cols = GQA * Q_BLOCK
    qt = qt_ref[0, 0, 0]
    b31 = b31_ref[0]
    lane_q = lax.broadcasted_iota(jnp.int32, (1, cols), 1) % Q_BLOCK

    n_cp = kcp_ref.shape[2]
    cpb = Q_BLOCK // CMP_STRIDE
    s_far = _dot(kcp_ref[0, 0], qt) + b31
    row = lax.broadcasted_iota(jnp.int32, (n_cp, cols), 0)
    s_far = jnp.where((row >= CMP_PAD) & (row < cpb * qb), s_far, NEG_INF)
    near0 = pl.multiple_of(cpb * qb, SUBLANES)
    s_near = _dot(kcp_ref[0, 0, pl.ds(near0, CMP_NEAR), :], qt) + bcn_ref[0]
    rown = lax.broadcasted_iota(jnp.int32, (CMP_NEAR, cols), 0)
    s_near = jnp.where(rown + cpb * qb >= CMP_PAD, s_near, NEG_INF)
    m_c = jnp.maximum(jnp.max(s_far, axis=0, keepdims=True), jnp.max(s_near, axis=0, keepdims=True))
    p_far = jnp.exp(s_far - m_c)
    p_near = jnp.exp(s_near - m_c)
    l_c = jnp.sum(p_far, axis=0, keepdims=True) + jnp.sum(p_near, axis=0, keepdims=True)
    any_valid = (q0 + lane_q >= CMP_LEN - 1).astype(F32)
    scale_c = any_valid / l_c
    pf_hi = p_far.astype(BF16)
    pn_hi = p_near.astype(BF16)
    vc_near = vcp_ref[0, 0, pl.ds(near0, CMP_NEAR), :]
    o_c = (_dot(vct_ref[0, 0], pf_hi) + _dot_tn(vc_near, pn_hi)) * scale_c
    pf_lo = (p_far - pf_hi.astype(F32)).astype(BF16)
    pn_lo = (p_near - pn_hi.astype(F32)).astype(BF16)
    ov_near = ov_ref[pl.ds(near0, CMP_NEAR), :]
    imp4 = (_dot(ovt_ref[...], pf_hi) + _dot(ovt_ref[...], pf_lo)
            + _dot_tn(ov_near, pn_hi) + _dot_tn(ov_near, pn_lo)) * scale_c
    imp = imp4[:, 0:Q_BLOCK]
    for g in range(1, GQA):
        imp = imp + imp4[:, g * Q_BLOCK:(g + 1) * Q_BLOCK]

    n_win = WINDOW + Q_BLOCK
    ks0 = pl.multiple_of(q0, Q_BLOCK)
    s_w = _dot(kw_ref[0, 0, pl.ds(ks0, n_win), :], qt) + bw_ref[0]
    roww = lax.broadcasted_iota(jnp.int32, (n_win, cols), 0)
    s_w = jnp.where(roww + q0 >= WINDOW, s_w, NEG_INF)
    p_w = jnp.exp(s_w - jnp.max(s_w, axis=0, keepdims=True)).astype(BF16)
    acc_w = _dot(vwt_ref[0, 0, :, pl.ds(ks0, n_win)], p_w)
    gates = gate_ref[0, 0, 0]
    ow_scr[...] = gates[0:1, :] * o_c + gates[2:3, :] * (acc_w[0:dh, :] / acc_w[dh:dh + 1, :])

    n_sb = imp.shape[0]
    blk = lax.broadcasted_iota(jnp.int32, (n_sb, Q_BLOCK), 0)
    qq = lax.broadcasted_iota(jnp.int32, (n_sb, Q_BLOCK), 1)
    cur = (q0 + qq) // SEL_BLOCK
    forced = (blk == 0) | (blk == cur) | (blk == cur - 1)
    v = jnp.where(forced, FORCE_SCORE, jnp.where(blk <= cur, imp, -1.0))
    sel = jnp.zeros((n_sb, Q_BLOCK), F32)
    for _ in range(min(SEL_TOPN, n_sb)):
        mx = jnp.max(v, axis=0, keepdims=True)
        idx = jnp.min(jnp.where(v == mx, blk, n_sb), axis=0, keepdims=True)
        hit = blk == idx
        sel = jnp.where(hit, 1.0, sel)
        v = jnp.where(hit, -jnp.inf, v)
    near_blk0 = 2 * qb - SEL_NEAR // SEL_BLOCK // 2
    neg_near = jnp.where(sel > 0.5, 0.0, NEG_INF)
    neg_far = jnp.where(blk < near_blk0, neg_near, NEG_INF).astype(BF16)
    neg_near = neg_near.astype(BF16)

    qa_scr[n_sb:n_sb + dh, :] = qt
    qa_scr[n_sb + dh:, :] = jnp.zeros((KA_WIDTH - n_sb - dh, cols), BF16)
    for g in range(GQA):
        qa_scr[:n_sb, g * Q_BLOCK:(g + 1) * Q_BLOCK] = neg_near
    s_n =_dot(ka_ref[0, 0, pl.ds(ks0, SEL_NEAR), :], qa_scr[...]) + bsn_ref[0]
    rowk = lax.broadcasted_iota(jnp.int32, (SEL_NEAR, cols), 0)
    s_n = jnp.where(rowk + q0 >= SEL_NEAR // 2, s_n, NEG_INF)
    m0 = jnp.max(s_n, axis=0, keepdims=True)
    m_scr[...] = m0
    acc_scr[...] = _dot(vst_ref[0, 0, :, pl.ds(ks0, SEL_NEAR)], jnp.exp(s_n - m0).astype(BF16))

    for g in range(GQA):
        qa_scr[:n_sb, g * Q_BLOCK:(g + 1) * Q_BLOCK] = neg_far
    qa_scr[n_sb + dh:n_sb + dh + BIAS_ROWS, :] = b31p_ref[0]
    n_far = jnp.maximum(q0 - SEL_NEAR // 2 + SEL_FAR_KEYS - 1, 0) // SEL_FAR_KEYS

    def far_scores(k0):
        return _dot(ka_ref[0, 0, pl.ds(k0, SEL_FAR_KEYS), :], qa_scr[...])

    def far_update(k0, s_f):
        m_old = m_scr[...]
        m_new = jnp.maximum(m_old, jnp.max(s_f, axis=0, keepdims=True))
        p_f = jnp.exp(s_f - m_new).astype(BF16)
        acc_scr[...] = (jnp.exp(m_old - m_new) * acc_scr[...]
                        + _dot(vst_ref[0, 0, :, pl.ds(k0, SEL_FAR_KEYS)], p_f))
        m_scr[...] = m_new

    def far_pair(t, carry):
        k0 = pl.multiple_of(SEL_NEAR // 2 + 2 * t * SEL_FAR_KEYS, SEL_NEAR // 2)
        k1 = pl.multiple_of(k0 + SEL_FAR_KEYS, SEL_NEAR // 2)
        k2 = pl.multiple_of(k1 + SEL_FAR_KEYS, SEL_NEAR // 2)
        sb_scr[...] = far_scores(k1)
        far_update(k0, sa_scr[...])
        sa_scr[...] = far_scores(k2)
        far_update(k1, sb_scr[...])
        return carry

    n_pairs = (n_far + 1) // 2

    @pl.when(n_pairs > 0)
    def _():
        sa_scr[...] = far_scores(SEL_NEAR // 2)

    lax.fori_loop(0, n_pairs, far_pair, 0)
    o_s = acc_scr[0:dh, :] / acc_scr[dh:dh + 1, :]

    o_ref[0, 0, 0] = (ow_scr[...] + gates[1:2, :] * o_s).astype(o_ref.dtype)


def _nsa(q_t, kc_pad, vc_pad, vc_t, ov_pad, ov_t, ka_pad, vs_t, kw_pad, vw_t, gates_t, b31, b31p, bcn, bw, bsn):
    B, HKV, NQB, dh, cols = q_t.shape
    per_bh = lambda b, h, i: (b, h, 0, 0)
    per_h = lambda b, h, i: (h, 0, 0)
    whole = lambda b, h, i: (0, 0)
    per_blk = lambda b, h, i: (b, h, i, 0, 0)
    bh_spec = lambda a: pl.BlockSpec((1, 1) + a.shape[2:], per_bh)
    h_spec = lambda a: pl.BlockSpec((1,) + a.shape[1:], per_h)
    return pl.pallas_call(
        functools.partial(_nsa_kernel, dh=dh),
        grid=(B, HKV, NQB),
        in_specs=[pl.BlockSpec((1, 1, 1, dh, cols), per_blk),
                  bh_spec(kc_pad), bh_spec(vc_pad), bh_spec(vc_t),
                  pl.BlockSpec(ov_pad.shape, whole), pl.BlockSpec(ov_t.shape, whole),
                  bh_spec(ka_pad), bh_spec(vs_t), bh_spec(kw_pad), bh_spec(vw_t),
                  pl.BlockSpec((1, 1, 1) + gates_t.shape[3:], per_blk),
                  h_spec(b31), h_spec(b31p), h_spec(bcn), h_spec(bw), h_spec(bsn)],
        out_specs=pl.BlockSpec((1, 1, 1, dh, cols), per_blk),
        out_shape=jax.ShapeDtypeStruct((B, HKV, NQB, dh, cols), BF16),
        scratch_shapes=[pltpu.VMEM((KA_WIDTH, cols), BF16), pltpu.VMEM((1, cols), F32),
                        pltpu.VMEM((vs_t.shape[2], cols), F32), pltpu.VMEM((dh, cols), F32),
                        pltpu.VMEM((SEL_FAR_KEYS, cols), F32), pltpu.VMEM((SEL_FAR_KEYS, cols), F32)],
        compiler_params=_cparams("arbitrary", "arbitrary", "arbitrary"),
        name="nsa_attention",
    )(q_t, kc_pad, vc_pad, vc_t, ov_pad, ov_t, ka_pad, vs_t, kw_pad, vw_t, gates_t, b31, b31p, bcn, bw, bsn)


def _attention(att, gates, k_cmp, v_cmp, rel_bias, B, S, dh):
    HKV = N_KV_HEADS
    H = N_HEADS
    nqb = S // Q_BLOCK
    n_sb = S // SEL_BLOCK
    assert n_sb == LANES, "selection blocks are laid out on one lane tile"
    kvw = HKV * dh
    c0 = H * dh

    def kv_heads(j):
        a = att[:, c0 + j * kvw:c0 + (j + 1) * kvw].reshape(B, S, HKV, dh)
        return a.transpose(0, 2, 1, 3)

    def values_t(v, n_front, n_back):
        vt = jnp.concatenate([v.transpose(0, 1, 3, 2), jnp.ones((B, HKV, 1, S), BF16),
                              jnp.zeros((B, HKV, V_ROWS - dh - 1, S), BF16)], axis=2)
        return jnp.pad(vt, ((0, 0), (0, 0), (0, 0), (n_front, n_back)))

    q_t = (att[:, :c0].reshape(B, nqb, Q_BLOCK, HKV, GQA, dh).transpose(0, 3, 1, 5, 4, 2)
           .reshape(B, HKV, nqb, dh, GQA * Q_BLOCK))
    ks, vs, kw, vw = kv_heads(2), kv_heads(3), kv_heads(4), kv_heads(5)
    front = lambda a, n: jnp.pad(a, ((0, 0), (0, 0), (n, 0), (0, 0)))
    feat = np.zeros((S + SEL_FAR_KEYS, n_sb + BIAS_ROWS), np.float32)
    feat[np.arange(S), np.arange(S) // SEL_BLOCK] = 1.0
    feat[S:, n_sb - 1] = 1.0
    feat[:S, n_sb:n_sb + BIAS_PIECES] = 1.0
    feat = jnp.asarray(feat, BF16)
    ks_back = jnp.pad(ks, ((0, 0), (0, 0), (0, SEL_FAR_KEYS), (0, 0)))
    rows_k = S + SEL_FAR_KEYS
    ka = jnp.concatenate([jnp.broadcast_to(feat[:, :n_sb], (B, HKV, rows_k, n_sb)), ks_back,
                          jnp.broadcast_to(feat[:, n_sb:], (B, HKV, rows_k, BIAS_ROWS)),
                          jnp.zeros((B, HKV, rows_k, KA_WIDTH - n_sb - dh - BIAS_ROWS), BF16)], axis=-1)
    ka_pad = front(ka, SEL_NEAR // 2)
    vs_t = values_t(vs, SEL_NEAR // 2, SEL_FAR_KEYS)
    kw_pad = front(kw, WINDOW)
    vw_t = values_t(vw, WINDOW, 0)
    kc_pad = front(k_cmp, CMP_PAD)
    vc_pad = front(v_cmp, CMP_PAD)
    vc_t = vc_pad.transpose(0, 1, 3, 2)
    n_c = S // CMP_STRIDE
    c_start = np.arange(n_c)[:, None] * CMP_STRIDE
    s_start = np.arange(n_sb)[None, :] * SEL_BLOCK
    overlap = ((c_start < s_start + SEL_BLOCK) & (c_start + CMP_LEN > s_start)).astype(np.float32)
    overlap[n_c - 1] = 0.0
    overlap = np.concatenate([np.zeros((CMP_PAD, n_sb), np.float32), overlap])
    ov_pad = jnp.asarray(overlap, BF16)
    ov_t = jnp.asarray(overlap.T.copy(), BF16)

    gates_t = (gates[:, :3 * H].reshape(B, nqb, Q_BLOCK, HKV, GQA, 3).transpose(0, 3, 1, 5, 4, 2)
               .reshape(B, HKV, nqb, 3, GQA * Q_BLOCK))
    gates_t = jnp.pad(gates_t, ((0, 0), (0, 0), (0, 0), (0, SUBLANES - 3), (0, 0)))

    far = S + WINDOW
    bcn = _bias_table(rel_bias, CMP_STRIDE * CMP_PAD - (CMP_LEN - 1), CMP_NEAR, 0, far, row_step=CMP_STRIDE)
    bw = _bias_table(rel_bias, WINDOW, WINDOW + Q_BLOCK, 0, WINDOW)
    bsn = _bias_table(rel_bias, SEL_NEAR // 2, SEL_NEAR, 0, far)
    b31 = jnp.repeat(rel_bias.astype(F32)[NUM_BUCKETS - 1].reshape(HKV, GQA), Q_BLOCK, axis=1)[:, None, :]
    pieces, rest = [], b31
    for _ in range(BIAS_PIECES):
        piece = rest.astype(BF16)
        pieces.append(piece)
        rest = rest - piece.astype(F32)
    b31p = jnp.concatenate(pieces + [jnp.zeros((HKV, BIAS_ROWS - BIAS_PIECES, GQA * Q_BLOCK), BF16)], axis=1)

    o = _nsa(q_t, kc_pad, vc_pad, vc_t, ov_pad, ov_t, ka_pad, vs_t, kw_pad, vw_t, gates_t, b31, b31p, bcn, bw, bsn)
    o = o.reshape(B, HKV, nqb, dh, GQA, Q_BLOCK).transpose(0, 2, 5, 1, 4, 3)
    return o.reshape(B * S, H * dh)


def _outproj_kernel(h_ref, yr_ref, ya_ref, wr_ref, wa_ref, g_ref, b_ref, o_ref):
    mix = _dot(yr_ref[...], wr_ref[...]) + _dot(ya_ref[...], wa_ref[...])
    o_ref[...] = _layer_norm(ALPHA * h_ref[...] + mix, g_ref[...], b_ref[...])


def _outproj_ln(h, y_rnn, y_att, w_r, w_a, g, b):
    T, D = h.shape
    dr, da = y_rnn.shape[1], y_att.shape[1]
    tm = PROJ_ROWS
    row = lambda i: (i, 0)
    full = lambda i: (0, 0)
    return pl.pallas_call(
        _outproj_kernel,
        grid=(T // tm,),
        in_specs=[pl.BlockSpec((tm, D), row), pl.BlockSpec((tm, dr), row), pl.BlockSpec((tm, da), row),
                  pl.BlockSpec((dr, D), full), pl.BlockSpec((da, D), full),
                  pl.BlockSpec((1, D), full), pl.BlockSpec((1, D), full)],
        out_specs=pl.BlockSpec((tm, D), row),
        out_shape=jax.ShapeDtypeStruct((T, D), F32),
        compiler_params=_cparams("arbitrary"),
        name="outproj_ln",
    )(h, y_rnn, y_att, w_r, w_a, g, b)


def _route_kernel(h_ref, wh_ref, wl_ref, rb_ref, tri_ref, e_ref, p_ref, w_ref, cnt_ref, carry_scr, *, cols):
    i = pl.program_id(0)

    @pl.when(i == 0)
    def _():
        carry_scr[...] = jnp.zeros_like(carry_scr)

    h = h_ref[...]
    h_hi = h.astype(BF16)
    h_lo = (h - h_hi.astype(F32)).astype(BF16)
    logits = _dot_nt(wh_ref[...], h_hi) + (_dot_nt(wl_ref[...], h_hi) + _dot_nt(wh_ref[...], h_lo))
    scores = _sigmoid(logits)
    biased = scores + rb_ref[...]
    gsz = N_EXPERTS // N_GROUPS
    gscore = []
    for g in range(N_GROUPS):
        blk = biased[g * gsz:(g + 1) * gsz]
        m1 = jnp.max(blk, axis=0, keepdims=True)
        is_m1 = blk == m1
        n_m1 = jnp.sum(jnp.where(is_m1, 1.0, 0.0), axis=0, keepdims=True)
        m2 = jnp.max(jnp.where(is_m1, -jnp.inf, blk), axis=0, keepdims=True)
        gscore.append(m1 + jnp.where(n_m1 > 1.5, m1, m2))
    masked = []
    for g in range(N_GROUPS):
        rank = jnp.zeros_like(gscore[g])
        for o in range(N_GROUPS):
            if o == g:
                continue
            ahead = (gscore[o] > gscore[g]) | (gscore[o] == gscore[g]) if o < g else gscore[o] > gscore[g]
            rank = rank + jnp.where(ahead, 1.0, 0.0)
        keep = rank < TOPK_GROUPS - 0.5
        masked.append(jnp.where(keep, biased[g * gsz:(g + 1) * gsz], NEG_INF))
    v = jnp.concatenate(masked, axis=0)
    rows = lax.broadcasted_iota(jnp.int32, (N_EXPERTS, cols), 0)
    sel = jnp.zeros((N_EXPERTS, cols), F32)
    idxs, svals = [], []
    for _ in range(TOP_K):
        mx = jnp.max(v, axis=0, keepdims=True)
        idx = jnp.min(jnp.where(v == mx, rows, N_EXPERTS), axis=0, keepdims=True)
        hit = rows == idx
        idxs.append(idx)
        svals.append(jnp.sum(jnp.where(hit, scores, 0.0), axis=0, keepdims=True))
        sel = jnp.where(hit, 1.0, sel)
        v = jnp.where(hit, -jnp.inf, v)
    total = svals[0]
    for k in range(1, TOP_K):
        total = total + svals[k]
    prefix = _dot(sel.astype(BF16), tri_ref[...]) + carry_scr[:, 0:1]
    for k in range(TOP_K):
        hit = rows == idxs[k]
        e_ref[k:k + 1, :] = idxs[k]
        p_ref[k:k + 1, :] = jnp.sum(jnp.where(hit, prefix, 0.0), axis=0, keepdims=True).astype(jnp.int32)
        w_ref[k:k + 1, :] = svals[k] / total * ROUTED_SCALE
    new_carry = carry_scr[...] + jnp.sum(sel, axis=1, keepdims=True)
    carry_scr[...] = new_carry
    cnt_ref[...] = new_carry


def _route(h1, rw_hi, rw_lo, rb, tri):
    T, D = h1.shape
    cols = ROUTE_COLS
    full = lambda i: (0, 0)
    out_col = pl.BlockSpec((TOP_K, cols), lambda i: (0, i))
    return pl.pallas_call(
        functools.partial(_route_kernel, cols=cols),
        grid=(T // cols,),
        in_specs=[pl.BlockSpec((cols, D), lambda i: (i, 0)),
                  pl.BlockSpec((N_EXPERTS, D), full), pl.BlockSpec((N_EXPERTS, D), full),
                  pl.BlockSpec((N_EXPERTS, 1), full), pl.BlockSpec((cols, cols), full)],
        out_specs=[out_col, out_col, out_col, pl.BlockSpec((N_EXPERTS, LANES), full)],
        out_shape=[jax.ShapeDtypeStruct((TOP_K, T), jnp.int32), jax.ShapeDtypeStruct((TOP_K, T), jnp.int32),
                   jax.ShapeDtypeStruct((TOP_K, T), F32), jax.ShapeDtypeStruct((N_EXPERTS, LANES), F32)],
        scratch_shapes=[pltpu.VMEM((N_EXPERTS, LANES), F32)],
        compiler_params=_cparams("arbitrary"),
        name="router",
    )(h1, rw_hi, rw_lo, rb, tri)


def _tile_major(a, tile):
    k, t = a.shape
    return a.reshape(k, t // tile, tile).transpose(1, 0, 2).reshape(-1)


def _dest_kernel(pstart_ref, e_ref, p_ref, d_ref):
    e = e_ref[...]

    def add_start(x, acc):
        return acc + jnp.where(e == x, pstart_ref[x], 0)

    d_ref[...] = lax.fori_loop(0, N_EXPERTS, add_start, p_ref[...])


def _dest_rows(eidx, pos, pstart):
    k, T = eidx.shape
    blk = pl.BlockSpec((k, ROUTE_COLS), lambda i, ps: (0, i))
    return pl.pallas_call(
        _dest_kernel,
        grid_spec=pltpu.PrefetchScalarGridSpec(num_scalar_prefetch=1, grid=(T // ROUTE_COLS,),
                                               in_specs=[blk, blk], out_specs=blk),
        out_shape=jax.ShapeDtypeStruct((k, T), jnp.int32),
        compiler_params=_cparams("arbitrary"),
        name="moe_dest_rows",
    )(pstart, eidx, pos)


def _load_indices(d_hbm, d_smem, idx_sem, n, step):
    base = pl.multiple_of(step * n, n)
    cp = pltpu.make_async_copy(d_hbm.at[pl.ds(base, n)], d_smem, idx_sem)
    cp.start()
    cp.wait()


def _issue_rows(n, make_copy):
    def body(c, carry):
        for u in range(ISSUE_UNROLL):
            make_copy(c * ISSUE_UNROLL + u).start(priority=u % 2)
        return carry

    lax.fori_loop(0, n // ISSUE_UNROLL, body, 0)


def _token_tile(ref, t):
    return ref.at[pl.ds(pl.multiple_of(t * SUBLANES, SUBLANES), SUBLANES), :]


def _to_token_tiles(dst_ref, x, n):
    for c in range(SUBLANES):
        dst_ref[pl.ds(c, n, stride=SUBLANES), :] = x[:, c * LANES:(c + 1) * LANES]


def _from_token_tiles(src_ref, first, n, c):
    return src_ref[pl.ds(first * SUBLANES + c, n, stride=SUBLANES), :]


def _dispatch_kernel(h_ref, d_hbm, xs_out, d_smem, stage, idx_sem, row_sem, *, rows):
    n = TOP_K * rows
    _load_indices(d_hbm, d_smem, idx_sem, n, pl.program_id(0))
    _to_token_tiles(stage, h_ref[...], rows)

    def row_copy(j):
        return pltpu.make_async_copy(_token_tile(stage, j & (rows - 1)), _token_tile(xs_out, d_smem[j]), row_sem)

    _issue_rows(n, row_copy)
    whole = xs_out.at[pl.ds(0, n * SUBLANES), :]
    pltpu.make_async_copy(whole, whole, row_sem).wait()


def _dispatch(h1, d_flat, n_rows):
    T, D = h1.shape
    rows = DISPATCH_ROWS
    n = TOP_K * rows
    return pl.pallas_call(
        functools.partial(_dispatch_kernel, rows=rows),
        grid=(T // rows,),
        in_specs=[pl.BlockSpec((rows, D), lambda i: (i, 0)), pl.BlockSpec(memory_space=pl.ANY)],
        out_specs=pl.BlockSpec(memory_space=pl.ANY),
        scratch_shapes=[pltpu.SMEM((n,), jnp.int32), pltpu.VMEM((rows * SUBLANES, LANES), F32),
                        pltpu.SemaphoreType.DMA(()), pltpu.SemaphoreType.DMA(())],
        out_shape=jax.ShapeDtypeStruct((n_rows * SUBLANES, LANES), F32),
        compiler_params=_cparams("arbitrary"),
        name="moe_dispatch",
    )(h1, d_flat)


def _expert_kernel(be_ref, nv_ref, na_ref, x_ref, wg_ref, wu_ref, wd_ref, y_ref, xb, wgb, wub, wdb):
    del na_ref
    i = pl.program_id(0)
    bm = x_ref.shape[0] // SUBLANES

    @pl.when((i == 0) | (be_ref[i] != be_ref[jnp.maximum(i - 1, 0)]))
    def _():
        wgb[...] = wg_ref[0].astype(BF16)
        wub[...] = wu_ref[0].astype(BF16)
        wdb[...] = wd_ref[0].astype(BF16)

    @pl.when(nv_ref[i] > 0)
    def _():
        valid = lax.broadcasted_iota(jnp.int32, (bm, LANES), 0) < nv_ref[i]
        for c in range(SUBLANES):
            x_c = _from_token_tiles(x_ref, 0, bm, c)
            xb[:, c * LANES:(c + 1) * LANES] = jnp.where(valid, x_c, 0.0).astype(BF16)
        x = xb[...]
        g = _dot(x, wgb[...])
        u = _dot(x, wub[...])
        hmid = (g * _sigmoid(g)) * u
        _to_token_tiles(y_ref, _dot(hmid.astype(BF16), wdb[...]), bm)

    @pl.when(nv_ref[i] <= 0)
    def _():
        y_ref[...] = jnp.zeros_like(y_ref)


def _experts(xs, w_gate, w_up, w_down, block_e, n_valid, n_active):
    R = xs.shape[0] // SUBLANES
    E, D, de = w_gate.shape
    bm = EXPERT_ROWS
    x_map = lambda i, be, nv, na: (jnp.minimum(i, jnp.maximum(na[0] - 1, 0)), 0)
    w_map = lambda i, be, nv, na: (be[i], 0, 0)
    return pl.pallas_call(
        _expert_kernel,
        grid_spec=pltpu.PrefetchScalarGridSpec(
            num_scalar_prefetch=3,
            grid=(R // bm,),
            in_specs=[pl.BlockSpec((bm * SUBLANES, LANES), x_map), pl.BlockSpec((1, D, de), w_map),
                      pl.BlockSpec((1, D, de), w_map), pl.BlockSpec((1, de, D), w_map)],
            out_specs=pl.BlockSpec((bm * SUBLANES, LANES), lambda i, be, nv, na: (i, 0)),
            scratch_shapes=[pltpu.VMEM((bm, D), BF16), pltpu.VMEM((D, de), BF16),
                            pltpu.VMEM((D, de), BF16), pltpu.VMEM((de, D), BF16)]),
        out_shape=jax.ShapeDtypeStruct((R * SUBLANES, LANES), F32),
        compiler_params=_cparams("arbitrary"),
        name="moe_experts",
    )(block_e, n_valid, n_active, xs, w_gate, w_up, w_down)


def _combine_kernel(h_ref, wk_ref, d_hbm, ys_hbm, sg_ref, su_ref, sd_ref, g_ref, b_ref,
                    o_ref, d_smem, buf, idx_sem, row_sem, *, rows):
    i = pl.program_id(0)
    n = TOP_K * rows
    slot = lax.rem(i, 2)

    def fetch(step, into):
        _load_indices(d_hbm, d_smem, idx_sem, n, step)

        def row_copy(j):
            return pltpu.make_async_copy(_token_tile(ys_hbm, d_smem[j]), _token_tile(buf.at[into], j),
                                         row_sem.at[into])

        _issue_rows(n, row_copy)

    @pl.when(i == 0)
    def _():
        fetch(0, 0)

    @pl.when(i + 1 < pl.num_programs(0))
    def _():
        fetch(i + 1, 1 - slot)

    h = h_ref[...]
    hb = h.astype(BF16)
    g = _dot(hb, sg_ref[...])
    u = _dot(hb, su_ref[...])
    shared = _dot(((g * _sigmoid(g)) * u).astype(BF16), sd_ref[...])
    cur = buf.at[slot]
    pltpu.make_async_copy(ys_hbm.at[pl.ds(0, n * SUBLANES), :], cur, row_sem.at[slot]).wait()
    wk = wk_ref[...]
    cols = []
    for c in range(SUBLANES):
        acc = wk[:, 0:1] * _from_token_tiles(cur, 0, rows, c)
        for k in range(1, TOP_K):
            acc = acc + wk[:, k:k + 1] * _from_token_tiles(cur, k * rows, rows, c)
        cols.append(acc)
    routed = jnp.concatenate(cols, axis=1)
    o_ref[...] = _layer_norm(ALPHA * h + (routed + shared), g_ref[...], b_ref[...])


def _combine(h1, wk_t, d_flat, ys, sh_gate, sh_up, sh_down, g, b):
    T, D = h1.shape
    ds = sh_gate.shape[1]
    rows = COMBINE_ROWS
    n = TOP_K * rows
    full = lambda i: (0, 0)
    return pl.pallas_call(
        functools.partial(_combine_kernel, rows=rows),
        grid=(T // rows,),
        in_specs=[pl.BlockSpec((rows, D), lambda i: (i, 0)), pl.BlockSpec((rows, TOP_K), lambda i: (i, 0)),
                  pl.BlockSpec(memory_space=pl.ANY), pl.BlockSpec(memory_space=pl.ANY),
                  pl.BlockSpec((D, ds), full), pl.BlockSpec((D, ds), full), pl.BlockSpec((ds, D), full),
                  pl.BlockSpec((1, D), full), pl.BlockSpec((1, D), full)],
        out_specs=pl.BlockSpec((rows, D), lambda i: (i, 0)),
        scratch_shapes=[pltpu.SMEM((n,), jnp.int32), pltpu.VMEM((2, n * SUBLANES, LANES), F32),
                        pltpu.SemaphoreType.DMA(()), pltpu.SemaphoreType.DMA((2,))],
        out_shape=jax.ShapeDtypeStruct((T, D), F32),
        compiler_params=_cparams("arbitrary"),
        name="moe_combine",
    )(h1, wk_t, d_flat, ys, sh_gate, sh_up, sh_down, g, b)


def _moe(h1, router_w, router_b, w_gate, w_up, w_down, sh_gate, sh_up, sh_down, ln_g, ln_b):
    T, D = h1.shape
    assert D == SUBLANES * LANES, "a token row is moved as one (SUBLANES, LANES) tile"
    rw_t = router_w.T
    rw_hi = rw_t.astype(BF16)
    rw_lo = (rw_t - rw_hi.astype(F32)).astype(BF16)
    tri = jnp.asarray(np.triu(np.ones((ROUTE_COLS, ROUTE_COLS), np.float32), 1), BF16)
    eidx, pos, wk, cnt = _route(h1, rw_hi, rw_lo, router_b.reshape(N_EXPERTS, 1), tri)

    bm = EXPERT_ROWS
    counts = cnt[:, 0].astype(jnp.int32)
    padded = (counts + bm - 1) // bm * bm
    pend = jnp.cumsum(padded)
    pstart = (pend - padded).astype(jnp.int32)
    n_blocks = T * TOP_K // bm + N_EXPERTS
    n_active = (pend[-1:] // bm).astype(jnp.int32)
    block_e = jnp.clip(jnp.searchsorted(pend, jnp.arange(n_blocks, dtype=jnp.int32) * bm, side='right'),
                       0, N_EXPERTS - 1).astype(jnp.int32)

    n_valid = jnp.clip(counts[block_e] - (jnp.arange(n_blocks, dtype=jnp.int32) * bm - pstart[block_e]), 0, bm)
    dest = _dest_rows(eidx, pos, pstart)
    xs = _dispatch(h1, _tile_major(dest, DISPATCH_ROWS), n_blocks * bm)
    ys = _experts(xs, w_gate, w_up, w_down, block_e, n_valid.astype(jnp.int32), n_active)
    return _combine(h1, wk.T, _tile_major(dest, COMBINE_ROWS), ys,
                    sh_gate.astype(BF16), sh_up.astype(BF16), sh_down.astype(BF16),
                    ln_g.reshape(1, D), ln_b.reshape(1, D))


def _block_diag(w):
    n, c, d = w.shape
    eye = jnp.eye(n, dtype=w.dtype)
    return (eye[:, None, :, None] * w[:, :, None, :]).reshape(n * c, n * d)


def _mixer_and_ln(h_in_x, ln_g, ln_b, w_in, conv_w, conv_b, rg_w_a, rg_b_a, rg_w_x, rg_b_x, rg_lambda,
                  cmp_k, cmp_v, rel_bias, w_out, ln1_g, ln1_b, B, S, first):
    T, D = h_in_x.shape
    d_rnn = D // D_RNN_FRAC
    dh = (D - d_rnn) // N_HEADS
    kvw = N_KV_HEADS * dh
    n_att = N_HEADS * dh + 6 * kvw
    c_rg, c_att = 2 * d_rnn, 2 * d_rnn + n_att
    w_rg = w_in[:, :c_rg].astype(BF16)
    w_att = w_in[:, c_rg:c_att].astype(BF16)
    w_gl = jnp.pad(w_in[:, c_att:], ((0, 0), (0, LANES - 3 * N_HEADS))).astype(BF16)
    if first:
        g0, b0 = ln_g.reshape(1, D), ln_b.reshape(1, D)
    else:
        raise NotImplementedError("depth > 1")
    h, rg, att, gates = _ln_inproj(h_in_x, g0, b0, w_rg, w_att, w_gl, q_cols=N_HEADS * dh, q_scale=dh ** -0.5)

    y_rnn = _rglru(rg.reshape(B, S, c_rg), conv_w, conv_b.reshape(1, d_rnn),
                   _block_diag(rg_w_a).astype(BF16), rg_b_a.reshape(1, d_rnn),
                   _block_diag(rg_w_x).astype(BF16), rg_b_x.reshape(1, d_rnn), rg_lambda.reshape(1, d_rnn))

    nch = S // CMP_STRIDE
    c0 = N_HEADS * dh

    def chunks_of(j):
        a = att[:, c0 + j * kvw:c0 + (j + 1) * kvw].reshape(B, nch, CMP_STRIDE, N_KV_HEADS, dh)
        return a.transpose(0, 3, 1, 2, 4).reshape(B, N_KV_HEADS, nch, CMP_STRIDE * dh)

    chunks = jnp.stack([chunks_of(0), chunks_of(1)])
    half = CMP_STRIDE * dh
    stack = lambda i: jnp.stack([cmp_k[i], cmp_v[i]])
    pos = jnp.broadcast_to(stack(0).reshape(2, 1, CMP_LEN * dh), (2, SUBLANES, CMP_LEN * dh)).astype(BF16)
    w1 = stack(1).astype(BF16)
    kv_cmp = _compress(chunks, pos, w1[:, :half], w1[:, half:], stack(2)[:, None, :],
                       stack(3).astype(BF16), stack(4)[:, None, :])

    y_att = _attention(att, gates, kv_cmp[0], kv_cmp[1], rel_bias, B, S, dh)
    w_o = w_out.astype(BF16)
    return _outproj_ln(h, y_rnn.reshape(T, d_rnn), y_att, w_o[:d_rnn], w_o[d_rnn:],
                       ln1_g.reshape(1, D), ln1_b.reshape(1, D))


def kernel(x, ln_in_g, ln_in_b, w_in, conv_w, conv_b, rg_w_a, rg_b_a, rg_w_x, rg_b_x, rg_lambda, cmp_pos_k, cmp_k_w1, cmp_k_b1, cmp_k_w2, cmp_k_b2, cmp_pos_v, cmp_v_w1, cmp_v_b1, cmp_v_w2, cmp_v_b2, rel_bias, w_out, ln1_g, ln1_b, router_w, router_b, w_gate, w_up, w_down, sh_gate, sh_up, sh_down, ln2_g, ln2_b):
    B, S, D = x.shape
    assert w_in.shape[0] == DEPTH
    l = 0
    cmp_k = (cmp_pos_k[l], cmp_k_w1[l], cmp_k_b1[l], cmp_k_w2[l], cmp_k_b2[l])
    cmp_v = (cmp_pos_v[l], cmp_v_w1[l], cmp_v_b1[l], cmp_v_w2[l], cmp_v_b2[l])
    h1 = _mixer_and_ln(x.reshape(B * S, D), ln_in_g, ln_in_b, w_in[l], conv_w[l], conv_b[l], rg_w_a[l], rg_b_a[l],
                       rg_w_x[l], rg_b_x[l], rg_lambda[l], cmp_k, cmp_v, rel_bias, w_out[l], ln1_g[l], ln1_b[l],
                       B, S, True)
    out = _moe(h1, router_w[l], router_b[l], w_gate[l], w_up[l], w_down[l], sh_gate[l], sh_up[l], sh_down[l],
               ln2_g[l], ln2_b[l])
    return out.reshape(B, S, D)
```

```python
import functools
import math

import numpy as np
import jax
import jax.numpy as jnp
from jax import lax
from jax.experimental import pallas as pl
from jax.experimental.pallas import tpu as pltpu

F32 = jnp.float32
BF16 = jnp.bfloat16

DEPTH = 1
D_RNN_FRAC = 2
RG_BLOCKS = 8
CONV_W = 4
RG_C = 8.0
N_HEADS = 8
N_KV_HEADS = 2
GQA = N_HEADS // N_KV_HEADS
CMP_STRIDE = 16
CMP_LEN = 2 * CMP_STRIDE
SEL_BLOCK = 64
SEL_TOPN = 16
WINDOW = 512
Q_BLOCK = 128
NUM_BUCKETS = 32
MAX_DISTANCE = 128
N_EXPERTS = 256
TOP_K = 8
N_GROUPS = 8
TOPK_GROUPS = 4
ROUTED_SCALE = 2.5
ALPHA = (2 * DEPTH) ** 0.25
LN_EPS = 1e-5
NEG_INF = -1e30
FORCE_SCORE = 1e9

LANES = 128
SUBLANES = 8
VMEM_LIMIT_BYTES = 56 * 1024 * 1024

PROJ_ROWS = 512
RG_ROWS = 256
ROUTE_COLS = 512
DEST_COLS = 4096
DISPATCH_ROWS = 256
EXPERT_ROWS = 256
COMBINE_ROWS = 128
ISSUE_UNROLL = 8
SEL_FAR_KEYS = 512
CMP_NEAR = 32
CMP_PAD = CMP_NEAR - Q_BLOCK // CMP_STRIDE
SEL_NEAR = 2 * Q_BLOCK
KA_WIDTH = 256
BIAS_PIECES = 3
BIAS_ROWS = 16
HEADS_PER_TILE = 2
V_ROWS = 80


def _cparams(*sem):
    return pltpu.CompilerParams(dimension_semantics=sem, vmem_limit_bytes=VMEM_LIMIT_BYTES)


def _sigmoid(x):
    return 1.0 / (1.0 + jnp.exp(-x))


def _gelu_tanh(x):
    return 0.5 * x * (1.0 + jnp.tanh(math.sqrt(2.0 / math.pi) * (x + 0.044715 * (x * x * x))))


def _layer_norm(x, g, b):
    mu = jnp.mean(x, axis=-1, keepdims=True)
    xc = x - mu
    var = jnp.mean(xc * xc, axis=-1, keepdims=True)
    return xc * lax.rsqrt(var + LN_EPS) * g + b


def _dot(a, b):
    return jnp.dot(a, b, preferred_element_type=F32)


def _dot_nt(a, b):
    return lax.dot_general(a, b, (((1,), (1,)), ((), ())), preferred_element_type=F32)


def _head_slabs():
    for j in range(N_HEADS // HEADS_PER_TILE):
        yield j, (j * HEADS_PER_TILE) // GQA, (j * HEADS_PER_TILE) % GQA


def _ln_inproj_kernel(x_ref, g_ref, b_ref, wrg_ref, watt_ref, wgl_ref,
                      h_ref, rg_ref, kv_ref, qt_ref, gt_ref, *, q_cols, q_scale, dh):
    h = _layer_norm(x_ref[...], g_ref[...], b_ref[...])
    h_ref[...] = h
    hb = h.astype(BF16)
    rg_ref[...] = _dot(hb, wrg_ref[...])
    att = _dot(hb, watt_ref[...])
    kv_ref[...] = att[:, q_cols:].astype(BF16)
    q = att[:, :q_cols] * q_scale
    gates = _sigmoid(_dot(hb, wgl_ref[...]))
    for qbi in range(x_ref.shape[0] // Q_BLOCK):
        rows = slice(qbi * Q_BLOCK, (qbi + 1) * Q_BLOCK)
        for j, hk, g0 in _head_slabs():
            t = q[rows, j * LANES:(j + 1) * LANES].T
            for u in range(HEADS_PER_TILE):
                qt_ref[0, hk, qbi, :, (g0 + u) * Q_BLOCK:(g0 + u + 1) * Q_BLOCK] = (
                    t[u * dh:(u + 1) * dh].astype(BF16))
        gt = gates[rows, :].T
        for hk in range(N_KV_HEADS):
            gt_ref[0, hk, qbi, 3:, :] = jnp.zeros((SUBLANES - 3, GQA * Q_BLOCK), F32)
            for g in range(GQA):
                for c in range(3):
                    r = (hk * GQA + g) * 3 + c
                    gt_ref[0, hk, qbi, c:c + 1, g * Q_BLOCK:(g + 1) * Q_BLOCK] = gt[r:r + 1, :]


def _ln_inproj(x2, g, b, w_rg, w_att, w_gl, *, q_cols, q_scale, dh, B, S):
    T, D = x2.shape
    n_rg, n_kv, n_gl = w_rg.shape[1], w_att.shape[1] - q_cols, w_gl.shape[1]
    tm = PROJ_ROWS
    per_b = S // tm
    nq = tm // Q_BLOCK
    cols = GQA * Q_BLOCK
    row = lambda i: (i, 0)
    full = lambda i: (0, 0)
    blk = lambda i: (i // per_b, 0, i % per_b, 0, 0)
    return pl.pallas_call(
        functools.partial(_ln_inproj_kernel, q_cols=q_cols, q_scale=q_scale, dh=dh),
        grid=(T // tm,),
        in_specs=[pl.BlockSpec((tm, D), row), pl.BlockSpec((1, D), full), pl.BlockSpec((1, D), full),
                  pl.BlockSpec((D, n_rg), full), pl.BlockSpec((D, n_kv + q_cols), full),
                  pl.BlockSpec((D, n_gl), full)],
        out_specs=[pl.BlockSpec((tm, D), row), pl.BlockSpec((tm, n_rg), row), pl.BlockSpec((tm, n_kv), row),
                   pl.BlockSpec((1, N_KV_HEADS, nq, dh, cols), blk),
                   pl.BlockSpec((1, N_KV_HEADS, nq, SUBLANES, cols), blk)],
        out_shape=[jax.ShapeDtypeStruct((T, D), F32), jax.ShapeDtypeStruct((T, n_rg), F32),
                   jax.ShapeDtypeStruct((T, n_kv), BF16),
                   jax.ShapeDtypeStruct((B, N_KV_HEADS, S // Q_BLOCK, dh, cols), BF16),
                   jax.ShapeDtypeStruct((B, N_KV_HEADS, S // Q_BLOCK, SUBLANES, cols), F32)],
        compiler_params=_cparams("arbitrary"),
        name="ln_inproj",
    )(x2, g, b, w_rg, w_att, w_gl)


def _rglru_kernel(rg_ref, cw_ref, cb_ref, wa_ref, ba_ref, wx_ref, bx_ref, lam_ref, y_ref,
                  xprev_scr, hprev_scr, *, d_rnn, rows):
    s = pl.program_id(1)

    @pl.when(s == 0)
    def _():
        xprev_scr[...] = jnp.zeros_like(xprev_scr)
        hprev_scr[...] = jnp.zeros_like(hprev_scr)

    xr = rg_ref[0, :, :d_rnn]
    gr = rg_ref[0, :, d_rnn:]
    xcat = jnp.concatenate([xprev_scr[...], xr], axis=0)
    xc = cw_ref[CONV_W - 1:CONV_W, :] * xr + cb_ref[...]
    for k in range(1, CONV_W):
        shifted = pltpu.roll(xcat, k, 0)[SUBLANES:, :]
        xc = xc + cw_ref[CONV_W - 1 - k:CONV_W - k, :] * shifted
    xprev_scr[...] = xr[rows - SUBLANES:, :]

    xcb = xc.astype(BF16)
    r = _sigmoid(_dot(xcb, wa_ref[...]) + ba_ref[...])
    i = _sigmoid(_dot(xcb, wx_ref[...]) + bx_ref[...])
    neg_lam = -lam_ref[...]
    softplus = jnp.maximum(neg_lam, 0.0) + jnp.log1p(jnp.exp(-jnp.abs(neg_lam)))
    log_a = -RG_C * r * softplus
    a = jnp.exp(log_a)
    u = jnp.sqrt(jnp.tanh(-log_a) * (1.0 + a * a)) * (i * xc)

    t_idx = lax.broadcasted_iota(jnp.int32, (rows, d_rnn), 0)
    d = 1
    while d < rows:
        keep = t_idx >= d
        a_sh = jnp.where(keep, pltpu.roll(a, d, 0), 1.0)
        u_sh = jnp.where(keep, pltpu.roll(u, d, 0), 0.0)
        u = u + a * u_sh
        a = a * a_sh
        d *= 2
    h = u + a * hprev_scr[0:1, :]
    hprev_scr[...] = jnp.broadcast_to(h[rows - 1:rows, :], hprev_scr.shape)
    y_ref[0] = (h * _gelu_tanh(gr)).astype(y_ref.dtype)


def _rglru(rg3, conv_w, conv_b, wa_bd, b_a, wx_bd, b_x, lam):
    B, S, two_d = rg3.shape
    d = two_d // 2
    rows = RG_ROWS
    full = lambda b, s: (0, 0)
    return pl.pallas_call(
        functools.partial(_rglru_kernel, d_rnn=d, rows=rows),
        grid=(B, S // rows),
        in_specs=[pl.BlockSpec((1, rows, two_d), lambda b, s: (b, s, 0)),
                  pl.BlockSpec((CONV_W, d), full), pl.BlockSpec((1, d), full),
                  pl.BlockSpec((d, d), full), pl.BlockSpec((1, d), full),
                  pl.BlockSpec((d, d), full), pl.BlockSpec((1, d), full), pl.BlockSpec((1, d), full)],
        out_specs=pl.BlockSpec((1, rows, d), lambda b, s: (b, s, 0)),
        out_shape=jax.ShapeDtypeStruct((B, S, d), BF16),
        scratch_shapes=[pltpu.VMEM((SUBLANES, d), F32), pltpu.VMEM((SUBLANES, d), F32)],
        compiler_params=_cparams("arbitrary", "arbitrary"),
        name="rglru",
    )(rg3, conv_w, conv_b, wa_bd, b_a, wx_bd, b_x, lam)


def _compress_kernel(x_ref, pos_ref, w1a_ref, w1b_ref, b1_ref, w2_ref, b2_ref, o_ref, *, n_chunks):
    x = x_ref[0, 0, 0]
    w1a = w1a_ref[0]
    w1b = w1b_ref[0]
    pos = pos_ref[0]
    half = x.shape[1]
    pos_term = _dot(pos[:, :half], w1a) + _dot(pos[:, half:], w1b)
    first = _dot(x, w1a)
    second = _dot(x, w1b)
    hidden = first + pltpu.roll(second, n_chunks - 1, 0) + pos_term[0:1, :] + b1_ref[0]
    out = _dot(_gelu_tanh(hidden).astype(BF16), w2_ref[0]) + b2_ref[0]
    rows = lax.broadcasted_iota(jnp.int32, out.shape, 0)
    o_ref[0, 0, 0] = jnp.where(rows < n_chunks - 1, out, 0.0).astype(o_ref.dtype)


def _compress(chunks, pos, w1a, w1b, b1, w2, b2):
    two, B, HKV, NCH, half = chunks.shape
    hid = w1a.shape[-1]
    dh = w2.shape[-1]
    kv = lambda c, b, h: (c, 0, 0)
    return pl.pallas_call(
        functools.partial(_compress_kernel, n_chunks=NCH),
        grid=(two, B, HKV),
        in_specs=[pl.BlockSpec((1, 1, 1, NCH, half), lambda c, b, h: (c, b, h, 0, 0)),
                  pl.BlockSpec((1, SUBLANES, 2 * half), kv),
                  pl.BlockSpec((1, half, hid), kv), pl.BlockSpec((1, half, hid), kv),
                  pl.BlockSpec((1, 1, hid), kv), pl.BlockSpec((1, hid, dh), kv), pl.BlockSpec((1, 1, dh), kv)],
        out_specs=pl.BlockSpec((1, 1, 1, NCH, dh), lambda c, b, h: (c, b, h, 0, 0)),
        out_shape=jax.ShapeDtypeStruct((two, B, HKV, NCH, dh), BF16),
        compiler_params=_cparams("arbitrary", "arbitrary", "arbitrary"),
        name="compress_kv",
    )(chunks, pos, w1a, w1b, b1, w2, b2)


def _t5_bucket_np(dist):
    n = np.maximum(dist, 0)
    max_exact = NUM_BUCKETS // 2
    nf = np.maximum(n, 1).astype(np.float32)
    large = max_exact + (np.log(nf / np.float32(max_exact)) / np.float32(math.log(MAX_DISTANCE / max_exact))
                         * np.float32(NUM_BUCKETS - max_exact)).astype(np.int32)
    return np.where(n < max_exact, n, np.minimum(large, NUM_BUCKETS - 1)).astype(np.int32)


def _bias_table(rel_bias, offset, n_rows, lo, hi, row_step=1):
    R = n_rows * row_step
    P = R + Q_BLOCK - 1
    d = np.arange(P) + (offset - (R - 1))
    f = rel_bias.astype(F32)[jnp.asarray(_t5_bucket_np(d))]
    f = jnp.where(jnp.asarray((d >= lo) & (d < hi))[:, None], f, NEG_INF).T
    hankel = jnp.tile(f, (1, R + 1))[:, :R * (P + 1)].reshape(N_HEADS, R, P + 1)[:, :, :Q_BLOCK]
    tab = hankel[:, ::-1][:, ::row_step]
    return (tab.reshape(N_KV_HEADS, GQA, n_rows, Q_BLOCK).transpose(0, 2, 1, 3)
            .reshape(N_KV_HEADS, n_rows, GQA * Q_BLOCK))


def _dot_tn(a, b):
    return lax.dot_general(a, b, (((0,), (0,)), ((), ())), preferred_element_type=F32)


def _nsa_kernel(qt_ref, kcp_ref, vcp_ref, vct_ref, ov_ref, ovt_ref, ka_ref, vst_ref, kw_ref, vwt_ref, gate_ref,
                b31_ref, b31p_ref, bcn_ref, bw_ref, bsn_ref, o_ref, qa_scr, m_scr, acc_scr, ow_scr,
                sa_scr, sb_scr, *, dh):
    qb = pl.program_id(2)
    q0 = qb * Q_BLOCK
    cols = GQA * Q_BLOCK
    qt = qt_ref[0, 0, 0]
    b31 = b31_ref[0]
    lane_q = lax.broadcasted_iota(jnp.int32, (1, cols), 1) % Q_BLOCK

    n_cp = kcp_ref.shape[2]
    cpb = Q_BLOCK // CMP_STRIDE
    s_far = _dot(kcp_ref[0, 0], qt) + b31
    row = lax.broadcasted_iota(jnp.int32, (n_cp, cols), 0)
    s_far = jnp.where((row >= CMP_PAD) & (row < cpb * qb), s_far, NEG_INF)
    near0 = pl.multiple_of(cpb * qb, SUBLANES)
    s_near = _dot(kcp_ref[0, 0, pl.ds(near0, CMP_NEAR), :], qt) + bcn_ref[0]
    rown = lax.broadcasted_iota(jnp.int32, (CMP_NEAR, cols), 0)
    s_near = jnp.where(rown + cpb * qb >= CMP_PAD, s_near, NEG_INF)
    m_c = jnp.maximum(jnp.max(s_far, axis=0, keepdims=True), jnp.max(s_near, axis=0, keepdims=True))
    p_far = jnp.exp(s_far - m_c)
    p_near = jnp.exp(s_near - m_c)
    l_c = jnp.sum(p_far, axis=0, keepdims=True) + jnp.sum(p_near, axis=0, keepdims=True)
    any_valid = (q0 + lane_q >= CMP_LEN - 1).astype(F32)
    scale_c = any_valid / l_c
    pf_hi = p_far.astype(BF16)
    pn_hi = p_near.astype(BF16)
    vc_near = vcp_ref[0, 0, pl.ds(near0, CMP_NEAR), :]
    o_c = (_dot(vct_ref[0, 0], pf_hi) + _dot_tn(vc_near, pn_hi)) * scale_c
    pf_lo = (p_far - pf_hi.astype(F32)).astype(BF16)
    pn_lo = (p_near - pn_hi.astype(F32)).astype(BF16)
    ov_near = ov_ref[pl.ds(near0, CMP_NEAR), :]
    imp4 = (_dot(ovt_ref[...], pf_hi) + _dot(ovt_ref[...], pf_lo)
            + _dot_tn(ov_near, pn_hi) + _dot_tn(ov_near, pn_lo)) * scale_c
    imp = imp4[:, 0:Q_BLOCK]
    for g in range(1, GQA):
        imp = imp + imp4[:, g * Q_BLOCK:(g + 1) * Q_BLOCK]

    n_win = WINDOW + Q_BLOCK
    ks0 = pl.multiple_of(q0, Q_BLOCK)
    s_w = _dot(kw_ref[0, 0, pl.ds(ks0, n_win), :], qt) + bw_ref[0]
    roww = lax.broadcasted_iota(jnp.int32, (n_win, cols), 0)
    s_w = jnp.where(roww + q0 >= WINDOW, s_w, NEG_INF)
    p_w = jnp.exp(s_w - jnp.max(s_w, axis=0, keepdims=True)).astype(BF16)
    acc_w = _dot(vwt_ref[0, 0, :, pl.ds(ks0, n_win)], p_w)
    gates = gate_ref[0, 0, 0]
    ow_scr[...] = gates[0:1, :] * o_c + gates[2:3, :] * (acc_w[0:dh, :] / acc_w[dh:dh + 1, :])

    n_sb = imp.shape[0]
    blk = lax.broadcasted_iota(jnp.int32, (n_sb, Q_BLOCK), 0)
    qq = lax.broadcasted_iota(jnp.int32, (n_sb, Q_BLOCK), 1)
    cur = (q0 + qq) // SEL_BLOCK
    forced = (blk == 0) | (blk == cur) | (blk == cur - 1)
    v = jnp.where(forced, FORCE_SCORE, jnp.where(blk <= cur, imp, -1.0))
    sel = jnp.zeros((n_sb, Q_BLOCK), F32)
    for _ in range(min(SEL_TOPN, n_sb)):
        mx = jnp.max(v, axis=0, keepdims=True)
        idx = jnp.min(jnp.where(v == mx, blk, n_sb), axis=0, keepdims=True)
        hit = blk == idx
        sel = jnp.where(hit, 1.0, sel)
        v = jnp.where(hit, -jnp.inf, v)
    near_blk0 = 2 * qb - SEL_NEAR // SEL_BLOCK // 2
    neg_near = jnp.where(sel > 0.5, 0.0, NEG_INF)
    neg_far = jnp.where(blk < near_blk0, neg_near, NEG_INF).astype(BF16)
    neg_near = neg_near.astype(BF16)

    qa_scr[n_sb:n_sb + dh, :] = qt
    qa_scr[n_sb + dh:, :] = jnp.zeros((KA_WIDTH - n_sb - dh, cols), BF16)
    for g in range(GQA):
        qa_scr[:n_sb, g * Q_BLOCK:(g + 1) * Q_BLOCK] = neg_near
    s_n =_dot(ka_ref[0, 0, pl.ds(ks0, SEL_NEAR), :], qa_scr[...]) + bsn_ref[0]
    rowk = lax.broadcasted_iota(jnp.int32, (SEL_NEAR, cols), 0)
    s_n = jnp.where(rowk + q0 >= SEL_NEAR // 2, s_n, NEG_INF)
    m0 = jnp.max(s_n, axis=0, keepdims=True)
    m_scr[...] = m0
    acc_scr[...] = _dot(vst_ref[0, 0, :, pl.ds(ks0, SEL_NEAR)], jnp.exp(s_n - m0).astype(BF16))

    for g in range(GQA):
        qa_scr[:n_sb, g * Q_BLOCK:(g + 1) * Q_BLOCK] = neg_far
    qa_scr[n_sb + dh:n_sb + dh + BIAS_ROWS, :] = b31p_ref[0]
    n_far = jnp.maximum(q0 - SEL_NEAR // 2 + SEL_FAR_KEYS - 1, 0) // SEL_FAR_KEYS

    def far_scores(k0):
        return _dot(ka_ref[0, 0, pl.ds(k0, SEL_FAR_KEYS), :], qa_scr[...])

    def far_update(k0, s_f):
        m_old = m_scr[...]
        m_new = jnp.maximum(m_old, jnp.max(s_f, axis=0, keepdims=True))
        p_f = jnp.exp(s_f - m_new).astype(BF16)
        acc_scr[...] = (jnp.exp(m_old - m_new) * acc_scr[...]
                        + _dot(vst_ref[0, 0, :, pl.ds(k0, SEL_FAR_KEYS)], p_f))
        m_scr[...] = m_new

    def far_pair(t, carry):
        k0 = pl.multiple_of(SEL_NEAR // 2 + 2 * t * SEL_FAR_KEYS, SEL_NEAR // 2)
        k1 = pl.multiple_of(k0 + SEL_FAR_KEYS, SEL_NEAR // 2)
        k2 = pl.multiple_of(k1 + SEL_FAR_KEYS, SEL_NEAR // 2)
        sb_scr[...] = far_scores(k1)
        far_update(k0, sa_scr[...])
        sa_scr[...] = far_scores(k2)
        far_update(k1, sb_scr[...])
        return carry

    n_pairs = (n_far + 1) // 2

    @pl.when(n_pairs > 0)
    def _():
        sa_scr[...] = far_scores(SEL_NEAR // 2)

    lax.fori_loop(0, n_pairs, far_pair, 0)
    o_s = acc_scr[0:dh, :] / acc_scr[dh:dh + 1, :]

    o_ref[0, 0, 0] = (ow_scr[...] + gates[1:2, :] * o_s).astype(o_ref.dtype)


def _nsa(q_t, kc_pad, vc_pad, vc_t, ov_pad, ov_t, ka_pad, vs_t, kw_pad, vw_t, gates_t, b31, b31p, bcn, bw, bsn):
    B, HKV, NQB, dh, cols = q_t.shape
    per_bh = lambda b, h, i: (b, h, 0, 0)
    per_h = lambda b, h, i: (h, 0, 0)
    whole = lambda b, h, i: (0, 0)
    per_blk = lambda b, h, i: (b, h, i, 0, 0)
    bh_spec = lambda a: pl.BlockSpec((1, 1) + a.shape[2:], per_bh)
    h_spec = lambda a: pl.BlockSpec((1,) + a.shape[1:], per_h)
    return pl.pallas_call(
        functools.partial(_nsa_kernel, dh=dh),
        grid=(B, HKV, NQB),
        in_specs=[pl.BlockSpec((1, 1, 1, dh, cols), per_blk),
                  bh_spec(kc_pad), bh_spec(vc_pad), bh_spec(vc_t),
                  pl.BlockSpec(ov_pad.shape, whole), pl.BlockSpec(ov_t.shape, whole),
                  bh_spec(ka_pad), bh_spec(vs_t), bh_spec(kw_pad), bh_spec(vw_t),
                  pl.BlockSpec((1, 1, 1) + gates_t.shape[3:], per_blk),
                  h_spec(b31), h_spec(b31p), h_spec(bcn), h_spec(bw), h_spec(bsn)],
        out_specs=pl.BlockSpec((1, 1, 1, dh, cols), per_blk),
        out_shape=jax.ShapeDtypeStruct((B, HKV, NQB, dh, cols), BF16),
        scratch_shapes=[pltpu.VMEM((KA_WIDTH, cols), BF16), pltpu.VMEM((1, cols), F32),
                        pltpu.VMEM((vs_t.shape[2], cols), F32), pltpu.VMEM((dh, cols), F32),
                        pltpu.VMEM((SEL_FAR_KEYS, cols), F32), pltpu.VMEM((SEL_FAR_KEYS, cols), F32)],
        compiler_params=_cparams("arbitrary", "arbitrary", "arbitrary"),
        name="nsa_attention",
    )(q_t, kc_pad, vc_pad, vc_t, ov_pad, ov_t, ka_pad, vs_t, kw_pad, vw_t, gates_t, b31, b31p, bcn, bw, bsn)


def _attention(kv, q_t, gates_t, k_cmp, v_cmp, rel_bias, B, S, dh):
    HKV = N_KV_HEADS
    n_sb = S // SEL_BLOCK
    assert n_sb == LANES, "selection blocks are laid out on one lane tile"
    kvw = HKV * dh

    def kv_heads(j):
        a = kv[:, j * kvw:(j + 1) * kvw].reshape(B, S, HKV, dh)
        return a.transpose(0, 2, 1, 3)

    def values_t(v, n_front, n_back):
        vt = jnp.concatenate([v.transpose(0, 1, 3, 2), jnp.ones((B, HKV, 1, S), BF16),
                              jnp.zeros((B, HKV, V_ROWS - dh - 1, S), BF16)], axis=2)
        return jnp.pad(vt, ((0, 0), (0, 0), (0, 0), (n_front, n_back)))

    ks, vs, kw, vw = kv_heads(2), kv_heads(3), kv_heads(4), kv_heads(5)
    front = lambda a, n: jnp.pad(a, ((0, 0), (0, 0), (n, 0), (0, 0)))
    feat = np.zeros((S + SEL_FAR_KEYS, n_sb + BIAS_ROWS), np.float32)
    feat[np.arange(S), np.arange(S) // SEL_BLOCK] = 1.0
    feat[S:, n_sb - 1] = 1.0
    feat[:S, n_sb:n_sb + BIAS_PIECES] = 1.0
    feat = jnp.asarray(feat, BF16)
    ks_back = jnp.pad(ks, ((0, 0), (0, 0), (0, SEL_FAR_KEYS), (0, 0)))
    rows_k = S + SEL_FAR_KEYS
    ka = jnp.concatenate([jnp.broadcast_to(feat[:, :n_sb], (B, HKV, rows_k, n_sb)), ks_back,
                          jnp.broadcast_to(feat[:, n_sb:], (B, HKV, rows_k, BIAS_ROWS)),
                          jnp.zeros((B, HKV, rows_k, KA_WIDTH - n_sb - dh - BIAS_ROWS), BF16)], axis=-1)
    ka_pad = front(ka, SEL_NEAR // 2)
    vs_t = values_t(vs, SEL_NEAR // 2, SEL_FAR_KEYS)
    kw_pad = front(kw, WINDOW)
    vw_t = values_t(vw, WINDOW, 0)
    kc_pad = front(k_cmp, CMP_PAD)
    vc_pad = front(v_cmp, CMP_PAD)
    vc_t = vc_pad.transpose(0, 1, 3, 2)
    n_c = S // CMP_STRIDE
    c_start = np.arange(n_c)[:, None] * CMP_STRIDE
    s_start = np.arange(n_sb)[None, :] * SEL_BLOCK
    overlap = ((c_start < s_start + SEL_BLOCK) & (c_start + CMP_LEN > s_start)).astype(np.float32)
    overlap[n_c - 1] = 0.0
    overlap = np.concatenate([np.zeros((CMP_PAD, n_sb), np.float32), overlap])
    ov_pad = jnp.asarray(overlap, BF16)
    ov_t = jnp.asarray(overlap.T.copy(), BF16)


    far = S + WINDOW
    bcn = _bias_table(rel_bias, CMP_STRIDE * CMP_PAD - (CMP_LEN - 1), CMP_NEAR, 0, far, row_step=CMP_STRIDE)
    bw = _bias_table(rel_bias, WINDOW, WINDOW + Q_BLOCK, 0, WINDOW)
    bsn = _bias_table(rel_bias, SEL_NEAR // 2, SEL_NEAR, 0, far)
    b31 = jnp.repeat(rel_bias.astype(F32)[NUM_BUCKETS - 1].reshape(HKV, GQA), Q_BLOCK, axis=1)[:, None, :]
    pieces, rest = [], b31
    for _ in range(BIAS_PIECES):
        piece = rest.astype(BF16)
        pieces.append(piece)
        rest = rest - piece.astype(F32)
    b31p = jnp.concatenate(pieces + [jnp.zeros((HKV, BIAS_ROWS - BIAS_PIECES, GQA * Q_BLOCK), BF16)], axis=1)

    return _nsa(q_t, kc_pad, vc_pad, vc_t, ov_pad, ov_t, ka_pad, vs_t, kw_pad, vw_t, gates_t, b31, b31p, bcn, bw, bsn)


def _outproj_kernel(h_ref, yr_ref, yat_ref, wr_ref, wa_ref, g_ref, b_ref, o_ref, *, dh):
    blocks = []
    for qbi in range(yat_ref.shape[2]):
        tiles = []
        for j, hk, g0 in _head_slabs():
            slabs = [yat_ref[0, hk, qbi, :, (g0 + u) * Q_BLOCK:(g0 + u + 1) * Q_BLOCK].astype(F32)
                     for u in range(HEADS_PER_TILE)]
            tiles.append(jnp.concatenate(slabs, axis=0).T)
        blocks.append(jnp.concatenate(tiles, axis=1))
    y_att = jnp.concatenate(blocks, axis=0).astype(BF16)
    mix = _dot(yr_ref[...], wr_ref[...]) + _dot(y_att, wa_ref[...])
    o_ref[...] = _layer_norm(ALPHA * h_ref[...] + mix, g_ref[...], b_ref[...])


def _outproj_ln(h, y_rnn, y_att_t, w_r, w_a, g, b):
    T, D = h.shape
    dr, da = y_rnn.shape[1], w_a.shape[0]
    B, HKV, nqb, dh, cols = y_att_t.shape
    tm = PROJ_ROWS
    per_b = (nqb * Q_BLOCK) // tm
    nq = tm // Q_BLOCK
    row = lambda i: (i, 0)
    full = lambda i: (0, 0)
    return pl.pallas_call(
        functools.partial(_outproj_kernel, dh=dh),
        grid=(T // tm,),
        in_specs=[pl.BlockSpec((tm, D), row), pl.BlockSpec((tm, dr), row),
                  pl.BlockSpec((1, HKV, nq, dh, cols), lambda i: (i // per_b, 0, i % per_b, 0, 0)),
                  pl.BlockSpec((dr, D), full), pl.BlockSpec((da, D), full),
                  pl.BlockSpec((1, D), full), pl.BlockSpec((1, D), full)],
        out_specs=pl.BlockSpec((tm, D), row),
        out_shape=jax.ShapeDtypeStruct((T, D), F32),
        compiler_params=_cparams("arbitrary"),
        name="outproj_ln",
    )(h, y_rnn, y_att_t, w_r, w_a, g, b)


def _route_kernel(h_ref, wh_ref, wl_ref, rb_ref, tri_ref, e_ref, p_ref, w_ref, cnt_ref, carry_scr, *, cols):
    i = pl.program_id(0)

    @pl.when(i == 0)
    def _():
        carry_scr[...] = jnp.zeros_like(carry_scr)

    h = h_ref[...]
    h_hi = h.astype(BF16)
    h_lo = (h - h_hi.astype(F32)).astype(BF16)
    logits = _dot_nt(wh_ref[...], h_hi) + (_dot_nt(wl_ref[...], h_hi) + _dot_nt(wh_ref[...], h_lo))
    scores = _sigmoid(logits)
    biased = scores + rb_ref[...]
    gsz = N_EXPERTS // N_GROUPS
    gscore = []
    for g in range(N_GROUPS):
        blk = biased[g * gsz:(g + 1) * gsz]
        m1 = jnp.max(blk, axis=0, keepdims=True)
        is_m1 = blk == m1
        n_m1 = jnp.sum(jnp.where(is_m1, 1.0, 0.0), axis=0, keepdims=True)
        m2 = jnp.max(jnp.where(is_m1, -jnp.inf, blk), axis=0, keepdims=True)
        gscore.append(m1 + jnp.where(n_m1 > 1.5, m1, m2))
    masked = []
    for g in range(N_GROUPS):
        rank = jnp.zeros_like(gscore[g])
        for o in range(N_GROUPS):
            if o == g:
                continue
            ahead = (gscore[o] > gscore[g]) | (gscore[o] == gscore[g]) if o < g else gscore[o] > gscore[g]
            rank = rank + jnp.where(ahead, 1.0, 0.0)
        keep = rank < TOPK_GROUPS - 0.5
        masked.append(jnp.where(keep, biased[g * gsz:(g + 1) * gsz], NEG_INF))
    v = jnp.concatenate(masked, axis=0)
    rows = lax.broadcasted_iota(jnp.int32, (N_EXPERTS, cols), 0)
    sel = jnp.zeros((N_EXPERTS, cols), F32)
    idxs, svals = [], []
    for _ in range(TOP_K):
        mx = jnp.max(v, axis=0, keepdims=True)
        idx = jnp.min(jnp.where(v == mx, rows, N_EXPERTS), axis=0, keepdims=True)
        hit = rows == idx
        idxs.append(idx)
        svals.append(jnp.sum(jnp.where(hit, scores, 0.0), axis=0, keepdims=True))
        sel = jnp.where(hit, 1.0, sel)
        v = jnp.where(hit, -jnp.inf, v)
    total = svals[0]
    for k in range(1, TOP_K):
        total = total + svals[k]
    prefix = _dot(sel.astype(BF16), tri_ref[...]) + carry_scr[:, 0:1]
    for k in range(TOP_K):
        hit = rows == idxs[k]
        e_ref[k:k + 1, :] = idxs[k]
        p_ref[k:k + 1, :] = jnp.sum(jnp.where(hit, prefix, 0.0), axis=0, keepdims=True).astype(jnp.int32)
        w_ref[k:k + 1, :] = svals[k] / total * ROUTED_SCALE
    new_carry = carry_scr[...] + jnp.sum(sel, axis=1, keepdims=True)
    carry_scr[...] = new_carry
    cnt_ref[...] = new_carry


def _route(h1, rw_hi, rw_lo, rb, tri):
    T, D = h1.shape
    cols = ROUTE_COLS
    full = lambda i: (0, 0)
    out_col = pl.BlockSpec((TOP_K, cols), lambda i: (0, i))
    return pl.pallas_call(
        functools.partial(_route_kernel, cols=cols),
        grid=(T // cols,),
        in_specs=[pl.BlockSpec((cols, D), lambda i: (i, 0)),
                  pl.BlockSpec((N_EXPERTS, D), full), pl.BlockSpec((N_EXPERTS, D), full),
                  pl.BlockSpec((N_EXPERTS, 1), full), pl.BlockSpec((cols, cols), full)],
        out_specs=[out_col, out_col, out_col, pl.BlockSpec((N_EXPERTS, LANES), full)],
        out_shape=[jax.ShapeDtypeStruct((TOP_K, T), jnp.int32), jax.ShapeDtypeStruct((TOP_K, T), jnp.int32),
                   jax.ShapeDtypeStruct((TOP_K, T), F32), jax.ShapeDtypeStruct((N_EXPERTS, LANES), F32)],
        scratch_shapes=[pltpu.VMEM((N_EXPERTS, LANES), F32)],
        compiler_params=_cparams("arbitrary"),
        name="router",
    )(h1, rw_hi, rw_lo, rb, tri)


def _tile_major(a, tile):
    k, t = a.shape
    return a.reshape(k, t // tile, tile).transpose(1, 0, 2).reshape(-1)


def _dest_kernel(pstart_ref, e_ref, p_ref, d_ref):
    e = e_ref[...]

    def add_start(x, acc):
        return acc + jnp.where(e == x, pstart_ref[x], 0)

    d_ref[...] = lax.fori_loop(0, N_EXPERTS, add_start, p_ref[...])


def _dest_rows(eidx, pos, pstart):
    k, T = eidx.shape
    blk = pl.BlockSpec((k, DEST_COLS), lambda i, ps: (0, i))
    return pl.pallas_call(
        _dest_kernel,
        grid_spec=pltpu.PrefetchScalarGridSpec(num_scalar_prefetch=1, grid=(T // DEST_COLS,),
                                               in_specs=[blk, blk], out_specs=blk),
        out_shape=jax.ShapeDtypeStruct((k, T), jnp.int32),
        compiler_params=_cparams("arbitrary"),
        name="moe_dest_rows",
    )(pstart, eidx, pos)


def _load_indices(d_hbm, d_smem, idx_sem, n, step):
    base = pl.multiple_of(step * n, n)
    cp = pltpu.make_async_copy(d_hbm.at[pl.ds(base, n)], d_smem, idx_sem)
    cp.start()
    cp.wait()


def _issue_rows(n, make_copy):
    def body(c, carry):
        for u in range(ISSUE_UNROLL):
            make_copy(c * ISSUE_UNROLL + u).start(priority=u % 2)
        return carry

    lax.fori_loop(0, n // ISSUE_UNROLL, body, 0)


def _token_tile(ref, t):
    return ref.at[pl.ds(pl.multiple_of(t * SUBLANES, SUBLANES), SUBLANES), :]


def _to_token_tiles(dst_ref, x, n):
    for c in range(SUBLANES):
        dst_ref[pl.ds(c, n, stride=SUBLANES), :] = x[:, c * LANES:(c + 1) * LANES]


def _from_token_tiles(src_ref, first, n, c):
    return src_ref[pl.ds(first * SUBLANES + c, n, stride=SUBLANES), :]


def _dispatch_kernel(h_ref, d_hbm, xs_out, d_smem, stage, idx_sem, row_sem, *, rows):
    n = TOP_K * rows
    _load_indices(d_hbm, d_smem, idx_sem, n, pl.program_id(0))
    _to_token_tiles(stage, h_ref[...], rows)

    def row_copy(j):
        return pltpu.make_async_copy(_token_tile(stage, j & (rows - 1)), _token_tile(xs_out, d_smem[j]), row_sem)

    _issue_rows(n, row_copy)
    whole = xs_out.at[pl.ds(0, n * SUBLANES), :]
    pltpu.make_async_copy(whole, whole, row_sem).wait()


def _dispatch(h1, d_flat, n_rows):
    T, D = h1.shape
    rows = DISPATCH_ROWS
    n = TOP_K * rows
    return pl.pallas_call(
        functools.partial(_dispatch_kernel, rows=rows),
        grid=(T // rows,),
        in_specs=[pl.BlockSpec((rows, D), lambda i: (i, 0)), pl.BlockSpec(memory_space=pl.ANY)],
        out_specs=pl.BlockSpec(memory_space=pl.ANY),
        scratch_shapes=[pltpu.SMEM((n,), jnp.int32), pltpu.VMEM((rows * SUBLANES, LANES), F32),
                        pltpu.SemaphoreType.DMA(()), pltpu.SemaphoreType.DMA(())],
        out_shape=jax.ShapeDtypeStruct((n_rows * SUBLANES, LANES), F32),
        compiler_params=_cparams("arbitrary"),
        name="moe_dispatch",
    )(h1, d_flat)


def _expert_kernel(be_ref, nv_ref, na_ref, x_ref, wg_ref, wu_ref, wd_ref, y_ref, xb, wgb, wub, wdb):
    del na_ref
    i = pl.program_id(0)
    bm = x_ref.shape[0] // SUBLANES

    @pl.when((i == 0) | (be_ref[i] != be_ref[jnp.maximum(i - 1, 0)]))
    def _():
        wgb[...] = wg_ref[0].astype(BF16)
        wub[...] = wu_ref[0].astype(BF16)
        wdb[...] = wd_ref[0].astype(BF16)

    @pl.when(nv_ref[i] > 0)
    def _():
        valid = lax.broadcasted_iota(jnp.int32, (bm, LANES), 0) < nv_ref[i]
        for c in range(SUBLANES):
            x_c = _from_token_tiles(x_ref, 0, bm, c)
            xb[:, c * LANES:(c + 1) * LANES] = jnp.where(valid, x_c, 0.0).astype(BF16)
        x = xb[...]
        g = _dot(x, wgb[...])
        u = _dot(x, wub[...])
        hmid = (g * _sigmoid(g)) * u
        _to_token_tiles(y_ref, _dot(hmid.astype(BF16), wdb[...]), bm)

    @pl.when(nv_ref[i] <= 0)
    def _():
        y_ref[...] = jnp.zeros_like(y_ref)


def _experts(xs, w_gate, w_up, w_down, block_e, n_valid, n_active):
    R = xs.shape[0] // SUBLANES
    E, D, de = w_gate.shape
    bm = EXPERT_ROWS
    x_map = lambda i, be, nv, na: (jnp.minimum(i, jnp.maximum(na[0] - 1, 0)), 0)
    w_map = lambda i, be, nv, na: (be[i], 0, 0)
    return pl.pallas_call(
        _expert_kernel,
        grid_spec=pltpu.PrefetchScalarGridSpec(
            num_scalar_prefetch=3,
            grid=(R // bm,),
            in_specs=[pl.BlockSpec((bm * SUBLANES, LANES), x_map), pl.BlockSpec((1, D, de), w_map),
                      pl.BlockSpec((1, D, de), w_map), pl.BlockSpec((1, de, D), w_map)],
            out_specs=pl.BlockSpec((bm * SUBLANES, LANES), lambda i, be, nv, na: (i, 0)),
            scratch_shapes=[pltpu.VMEM((bm, D), BF16), pltpu.VMEM((D, de), BF16),
                            pltpu.VMEM((D, de), BF16), pltpu.VMEM((de, D), BF16)]),
        out_shape=jax.ShapeDtypeStruct((R * SUBLANES, LANES), F32),
        compiler_params=_cparams("arbitrary"),
        name="moe_experts",
    )(block_e, n_valid, n_active, xs, w_gate, w_up, w_down)


def _combine_kernel(h_ref, wk_ref, d_hbm, ys_hbm, sg_ref, su_ref, sd_ref, g_ref, b_ref,
                    o_ref, d_smem, buf, idx_sem, row_sem, *, rows):
    i = pl.program_id(0)
    n = TOP_K * rows
    slot = lax.rem(i, 2)

    def fetch(step, into):
        _load_indices(d_hbm, d_smem, idx_sem, n, step)

        def row_copy(j):
            return pltpu.make_async_copy(_token_tile(ys_hbm, d_smem[j]), _token_tile(buf.at[into], j),
                                         row_sem.at[into])

        _issue_rows(n, row_copy)

    @pl.when(i == 0)
    def _():
        fetch(0, 0)

    @pl.when(i + 1 < pl.num_programs(0))
    def _():
        fetch(i + 1, 1 - slot)

    h = h_ref[...]
    hb = h.astype(BF16)
    g = _dot(hb, sg_ref[...])
    u = _dot(hb, su_ref[...])
    shared = _dot(((g * _sigmoid(g)) * u).astype(BF16), sd_ref[...])
    cur = buf.at[slot]
    pltpu.make_async_copy(ys_hbm.at[pl.ds(0, n * SUBLANES), :], cur, row_sem.at[slot]).wait()
    wk = wk_ref[...]
    cols = []
    for c in range(SUBLANES):
        acc = wk[:, 0:1] * _from_token_tiles(cur, 0, rows, c)
        for k in range(1, TOP_K):
            acc = acc + wk[:, k:k + 1] * _from_token_tiles(cur, k * rows, rows, c)
        cols.append(acc)
    routed = jnp.concatenate(cols, axis=1)
    o_ref[...] = _layer_norm(ALPHA * h + (routed + shared), g_ref[...], b_ref[...])


def _combine(h1, wk_t, d_flat, ys, sh_gate, sh_up, sh_down, g, b):
    T, D = h1.shape
    ds = sh_gate.shape[1]
    rows = COMBINE_ROWS
    n = TOP_K * rows
    full = lambda i: (0, 0)
    return pl.pallas_call(
        functools.partial(_combine_kernel, rows=rows),
        grid=(T // rows,),
        in_specs=[pl.BlockSpec((rows, D), lambda i: (i, 0)), pl.BlockSpec((rows, TOP_K), lambda i: (i, 0)),
                  pl.BlockSpec(memory_space=pl.ANY), pl.BlockSpec(memory_space=pl.ANY),
                  pl.BlockSpec((D, ds), full), pl.BlockSpec((D, ds), full), pl.BlockSpec((ds, D), full),
                  pl.BlockSpec((1, D), full), pl.BlockSpec((1, D), full)],
        out_specs=pl.BlockSpec((rows, D), lambda i: (i, 0)),
        scratch_shapes=[pltpu.SMEM((n,), jnp.int32), pltpu.VMEM((2, n * SUBLANES, LANES), F32),
                        pltpu.SemaphoreType.DMA(()), pltpu.SemaphoreType.DMA((2,))],
        out_shape=jax.ShapeDtypeStruct((T, D), F32),
        compiler_params=_cparams("arbitrary"),
        name="moe_combine",
    )(h1, wk_t, d_flat, ys, sh_gate, sh_up, sh_down, g, b)


def _moe(h1, router_w, router_b, w_gate, w_up, w_down, sh_gate, sh_up, sh_down, ln_g, ln_b):
    T, D = h1.shape
    assert D == SUBLANES * LANES, "a token row is moved as one (SUBLANES, LANES) tile"
    rw_t = router_w.T
    rw_hi = rw_t.astype(BF16)
    rw_lo = (rw_t - rw_hi.astype(F32)).astype(BF16)
    tri = jnp.asarray(np.triu(np.ones((ROUTE_COLS, ROUTE_COLS), np.float32), 1), BF16)
    eidx, pos, wk, cnt = _route(h1, rw_hi, rw_lo, router_b.reshape(N_EXPERTS, 1), tri)

    bm = EXPERT_ROWS
    counts = cnt[:, 0].astype(jnp.int32)
    padded = (counts + bm - 1) // bm * bm
    pend = jnp.cumsum(padded)
    pstart = (pend - padded).astype(jnp.int32)
    n_blocks = T * TOP_K // bm + N_EXPERTS
    n_active = (pend[-1:] // bm).astype(jnp.int32)
    block_row0 = jnp.arange(n_blocks, dtype=jnp.int32) * bm
    block_e = jnp.minimum(jnp.sum((pend[None, :] <= block_row0[:, None]).astype(jnp.int32), axis=1), N_EXPERTS - 1)
    is_e = block_e[:, None] == jnp.arange(N_EXPERTS, dtype=jnp.int32)[None, :]
    pick = lambda v: jnp.sum(jnp.where(is_e, v[None, :], 0), axis=1)
    n_valid = jnp.clip(pick(counts) - (block_row0 - pick(pstart)), 0, bm)
    dest = _dest_rows(eidx, pos, pstart)
    xs = _dispatch(h1, _tile_major(dest, DISPATCH_ROWS), n_blocks * bm)
    ys = _experts(xs, w_gate, w_up, w_down, block_e, n_valid.astype(jnp.int32), n_active)
    return _combine(h1, wk.T, _tile_major(dest, COMBINE_ROWS), ys,
                    sh_gate.astype(BF16), sh_up.astype(BF16), sh_down.astype(BF16),
                    ln_g.reshape(1, D), ln_b.reshape(1, D))


def _block_diag(w):
    n, c, d = w.shape
    eye = jnp.eye(n, dtype=w.dtype)
    return (eye[:, None, :, None] * w[:, :, None, :]).reshape(n * c, n * d)


def _mixer_and_ln(h_in_x, ln_g, ln_b, w_in, conv_w, conv_b, rg_w_a, rg_b_a, rg_w_x, rg_b_x, rg_lambda,
                  cmp_k, cmp_v, rel_bias, w_out, ln1_g, ln1_b, B, S, first):
    T, D = h_in_x.shape
    d_rnn = D // D_RNN_FRAC
    dh = (D - d_rnn) // N_HEADS
    kvw = N_KV_HEADS * dh
    n_att = N_HEADS * dh + 6 * kvw
    c_rg, c_att = 2 * d_rnn, 2 * d_rnn + n_att
    w_rg = w_in[:, :c_rg].astype(BF16)
    w_att = w_in[:, c_rg:c_att].astype(BF16)
    w_gl = jnp.pad(w_in[:, c_att:], ((0, 0), (0, LANES - 3 * N_HEADS))).astype(BF16)
    if first:
        g0, b0 = ln_g.reshape(1, D), ln_b.reshape(1, D)
    else:
        raise NotImplementedError("depth > 1")
    assert HEADS_PER_TILE * dh == LANES and PROJ_ROWS % Q_BLOCK == 0 and S % PROJ_ROWS == 0
    h, rg, kv, q_t, gates_t = _ln_inproj(h_in_x, g0, b0, w_rg, w_att, w_gl, q_cols=N_HEADS * dh,
                                         q_scale=dh ** -0.5, dh=dh, B=B, S=S)

    y_rnn = _rglru(rg.reshape(B, S, c_rg), conv_w, conv_b.reshape(1, d_rnn),
                   _block_diag(rg_w_a).astype(BF16), rg_b_a.reshape(1, d_rnn),
                   _block_diag(rg_w_x).astype(BF16), rg_b_x.reshape(1, d_rnn), rg_lambda.reshape(1, d_rnn))

    nch = S // CMP_STRIDE

    def chunks_of(j):
        a = kv[:, j * kvw:(j + 1) * kvw].reshape(B, nch, CMP_STRIDE, N_KV_HEADS, dh)
        return a.transpose(0, 3, 1, 2, 4).reshape(B, N_KV_HEADS, nch, CMP_STRIDE * dh)

    chunks = jnp.stack([chunks_of(0), chunks_of(1)])
    half = CMP_STRIDE * dh
    stack = lambda i: jnp.stack([cmp_k[i], cmp_v[i]])
    pos = jnp.broadcast_to(stack(0).reshape(2, 1, CMP_LEN * dh), (2, SUBLANES, CMP_LEN * dh)).astype(BF16)
    w1 = stack(1).astype(BF16)
    kv_cmp = _compress(chunks, pos, w1[:, :half], w1[:, half:], stack(2)[:, None, :],
                       stack(3).astype(BF16), stack(4)[:, None, :])

    y_att = _attention(kv, q_t, gates_t, kv_cmp[0], kv_cmp[1], rel_bias, B, S, dh)
    w_o = w_out.astype(BF16)
    return _outproj_ln(h, y_rnn.reshape(T, d_rnn), y_att, w_o[:d_rnn], w_o[d_rnn:],
                       ln1_g.reshape(1, D), ln1_b.reshape(1, D))


def kernel(x, ln_in_g, ln_in_b, w_in, conv_w, conv_b, rg_w_a, rg_b_a, rg_w_x, rg_b_x, rg_lambda, cmp_pos_k, cmp_k_w1, cmp_k_b1, cmp_k_w2, cmp_k_b2, cmp_pos_v, cmp_v_w1, cmp_v_b1, cmp_v_w2, cmp_v_b2, rel_bias, w_out, ln1_g, ln1_b, router_w, router_b, w_gate, w_up, w_down, sh_gate, sh_up, sh_down, ln2_g, ln2_b):
    B, S, D = x.shape
    assert w_in.shape[0] == DEPTH
    l = 0
    cmp_k = (cmp_pos_k[l], cmp_k_w1[l], cmp_k_b1[l], cmp_k_w2[l], cmp_k_b2[l])
    cmp_v = (cmp_pos_v[l], cmp_v_w1[l], cmp_v_b1[l], cmp_v_w2[l], cmp_v_b2[l])
    h1 = _mixer_and_ln(x.reshape(B * S, D), ln_in_g, ln_in_b, w_in[l], conv_w[l], conv_b[l], rg_w_a[l], rg_b_a[l],
                       rg_w_x[l], rg_b_x[l], rg_lambda[l], cmp_k, cmp_v, rel_bias, w_out[l], ln1_g[l], ln1_b[l],
                       B, S, True)
    out = _moe(h1, router_w[l], router_b[l], w_gate[l], w_up[l], w_down[l], sh_gate[l], sh_up[l], sh_down[l],
               ln2_g[l], ln2_b[l])
    return out.reshape(B, S, D)
```

```python
import functools
import math

import numpy as np
import jax
import jax.numpy as jnp
from jax import lax
from jax.experimental import pallas as pl
from jax.experimental.pallas import tpu as pltpu

F32 = jnp.float32
BF16 = jnp.bfloat16

DEPTH = 1
D_RNN_FRAC = 2
RG_BLOCKS = 8
CONV_W = 4
RG_C = 8.0
N_HEADS = 8
N_KV_HEADS = 2
GQA = N_HEADS // N_KV_HEADS
CMP_STRIDE = 16
CMP_LEN = 2 * CMP_STRIDE
SEL_BLOCK = 64
SEL_TOPN = 16
WINDOW = 512
Q_BLOCK = 128
NUM_BUCKETS = 32
MAX_DISTANCE = 128
N_EXPERTS = 256
TOP_K = 8
N_GROUPS = 8
TOPK_GROUPS = 4
ROUTED_SCALE = 2.5
ALPHA = (2 * DEPTH) ** 0.25
LN_EPS = 1e-5
NEG_INF = -1e30
FORCE_SCORE = 1e9

LANES = 128
SUBLANES = 8
VMEM_LIMIT_BYTES = 56 * 1024 * 1024

PROJ_ROWS = 512
RG_ROWS = 256
ROUTE_COLS = 512
DEST_COLS = 4096
DISPATCH_ROWS = 512
EXPERT_ROWS = 256
COMBINE_ROWS = 256
ISSUE_UNROLL = 8
SEL_FAR_KEYS = 512
CMP_NEAR = 32
CMP_PAD = CMP_NEAR - Q_BLOCK // CMP_STRIDE
SEL_NEAR = 2 * Q_BLOCK
KA_WIDTH = 256
BIAS_PIECES = 3
BIAS_ROWS = 16
HEADS_PER_TILE = 2
V_ROWS = 80


def _cparams(*sem):
    return pltpu.CompilerParams(dimension_semantics=sem, vmem_limit_bytes=VMEM_LIMIT_BYTES)


def _sigmoid(x):
    return 1.0 / (1.0 + jnp.exp(-x))


def _gelu_tanh(x):
    return 0.5 * x * (1.0 + jnp.tanh(math.sqrt(2.0 / math.pi) * (x + 0.044715 * (x * x * x))))


def _layer_norm(x, g, b):
    mu = jnp.mean(x, axis=-1, keepdims=True)
    xc = x - mu
    var = jnp.mean(xc * xc, axis=-1, keepdims=True)
    return xc * lax.rsqrt(var + LN_EPS) * g + b


def _dot(a, b):
    return jnp.dot(a, b, preferred_element_type=F32)


def _dot_nt(a, b):
    return lax.dot_general(a, b, (((1,), (1,)), ((), ())), preferred_element_type=F32)


def _head_slabs():
    for j in range(N_HEADS // HEADS_PER_TILE):
        yield j, (j * HEADS_PER_TILE) // GQA, (j * HEADS_PER_TILE) % GQA


def _ln_inproj_kernel(x_ref, g_ref, b_ref, wrg_ref, watt_ref, wgl_ref,
                      h_ref, rg_ref, kv_ref, qt_ref, gt_ref, *, q_cols, q_scale, dh):
    h = _layer_norm(x_ref[...], g_ref[...], b_ref[...])
    h_ref[...] = h
    hb = h.astype(BF16)
    rg_ref[...] = _dot(hb, wrg_ref[...])
    att = _dot(hb, watt_ref[...])
    n_head_cols = kv_ref.shape[2]
    for hk in range(N_KV_HEADS):
        kv_ref[hk] = att[:, q_cols + hk * n_head_cols:q_cols + (hk + 1) * n_head_cols].astype(BF16)
    q = att[:, :q_cols] * q_scale
    gates = _sigmoid(_dot(hb, wgl_ref[...]))
    for qbi in range(x_ref.shape[0] // Q_BLOCK):
        rows = slice(qbi * Q_BLOCK, (qbi + 1) * Q_BLOCK)
        for j, hk, g0 in _head_slabs():
            t = q[rows, j * LANES:(j + 1) * LANES].T
            for u in range(HEADS_PER_TILE):
                qt_ref[0, hk, qbi, :, (g0 + u) * Q_BLOCK:(g0 + u + 1) * Q_BLOCK] = (
                    t[u * dh:(u + 1) * dh].astype(BF16))
        gt = gates[rows, :].T
        for hk in range(N_KV_HEADS):
            gt_ref[0, hk, qbi, 3:, :] = jnp.zeros((SUBLANES - 3, GQA * Q_BLOCK), F32)
            for g in range(GQA):
                for c in range(3):
                    r = (hk * GQA + g) * 3 + c
                    gt_ref[0, hk, qbi, c:c + 1, g * Q_BLOCK:(g + 1) * Q_BLOCK] = gt[r:r + 1, :]


def _ln_inproj(x2, g, b, w_rg, w_att, w_gl, *, q_cols, q_scale, dh, B, S):
    T, D = x2.shape
    n_rg, n_kv, n_gl = w_rg.shape[1], w_att.shape[1] - q_cols, w_gl.shape[1]
    tm = PROJ_ROWS
    per_b = S // tm
    nq = tm // Q_BLOCK
    cols = GQA * Q_BLOCK
    row = lambda i: (i, 0)
    full = lambda i: (0, 0)
    blk = lambda i: (i // per_b, 0, i % per_b, 0, 0)
    return pl.pallas_call(
        functools.partial(_ln_inproj_kernel, q_cols=q_cols, q_scale=q_scale, dh=dh),
        grid=(T // tm,),
        in_specs=[pl.BlockSpec((tm, D), row), pl.BlockSpec((1, D), full), pl.BlockSpec((1, D), full),
                  pl.BlockSpec((D, n_rg), full), pl.BlockSpec((D, n_kv + q_cols), full),
                  pl.BlockSpec((D, n_gl), full)],
        out_specs=[pl.BlockSpec((tm, D), row), pl.BlockSpec((tm, n_rg), row),
                   pl.BlockSpec((N_KV_HEADS, tm, n_kv // N_KV_HEADS), lambda i: (0, i, 0)),
                   pl.BlockSpec((1, N_KV_HEADS, nq, dh, cols), blk),
                   pl.BlockSpec((1, N_KV_HEADS, nq, SUBLANES, cols), blk)],
        out_shape=[jax.ShapeDtypeStruct((T, D), F32), jax.ShapeDtypeStruct((T, n_rg), F32),
                   jax.ShapeDtypeStruct((N_KV_HEADS, T, n_kv // N_KV_HEADS), BF16),
                   jax.ShapeDtypeStruct((B, N_KV_HEADS, S // Q_BLOCK, dh, cols), BF16),
                   jax.ShapeDtypeStruct((B, N_KV_HEADS, S // Q_BLOCK, SUBLANES, cols), F32)],
        compiler_params=_cparams("arbitrary"),
        name="ln_inproj",
    )(x2, g, b, w_rg, w_att, w_gl)


def _rglru_kernel(rg_ref, cw_ref, cb_ref, wa_ref, ba_ref, wx_ref, bx_ref, lam_ref, y_ref,
                  xprev_scr, hprev_scr, *, d_rnn, rows):
    s = pl.program_id(1)

    @pl.when(s == 0)
    def _():
        xprev_scr[...] = jnp.zeros_like(xprev_scr)
        hprev_scr[...] = jnp.zeros_like(hprev_scr)

    xr = rg_ref[0, :, :d_rnn]
    gr = rg_ref[0, :, d_rnn:]
    xcat = jnp.concatenate([xprev_scr[...], xr], axis=0)
    xc = cw_ref[CONV_W - 1:CONV_W, :] * xr + cb_ref[...]
    for k in range(1, CONV_W):
        shifted = pltpu.roll(xcat, k, 0)[SUBLANES:, :]
        xc = xc + cw_ref[CONV_W - 1 - k:CONV_W - k, :] * shifted
    xprev_scr[...] = xr[rows - SUBLANES:, :]

    xcb = xc.astype(BF16)
    r = _sigmoid(_dot(xcb, wa_ref[...]) + ba_ref[...])
    i = _sigmoid(_dot(xcb, wx_ref[...]) + bx_ref[...])
    neg_lam = -lam_ref[...]
    softplus = jnp.maximum(neg_lam, 0.0) + jnp.log1p(jnp.exp(-jnp.abs(neg_lam)))
    log_a = -RG_C * r * softplus
    a = jnp.exp(log_a)
    u = jnp.sqrt(jnp.tanh(-log_a) * (1.0 + a * a)) * (i * xc)

    t_idx = lax.broadcasted_iota(jnp.int32, (rows, d_rnn), 0)
    d = 1
    while d < rows:
        keep = t_idx >= d
        a_sh = jnp.where(keep, pltpu.roll(a, d, 0), 1.0)
        u_sh = jnp.where(keep, pltpu.roll(u, d, 0), 0.0)
        u = u + a * u_sh
        a = a * a_sh
        d *= 2
    h = u + a * hprev_scr[0:1, :]
    hprev_scr[...] = jnp.broadcast_to(h[rows - 1:rows, :], hprev_scr.shape)
    y_ref[0] = (h * _gelu_tanh(gr)).astype(y_ref.dtype)


def _rglru(rg3, conv_w, conv_b, wa_bd, b_a, wx_bd, b_x, lam):
    B, S, two_d = rg3.shape
    d = two_d // 2
    rows = RG_ROWS
    full = lambda b, s: (0, 0)
    return pl.pallas_call(
        functools.partial(_rglru_kernel, d_rnn=d, rows=rows),
        grid=(B, S // rows),
        in_specs=[pl.BlockSpec((1, rows, two_d), lambda b, s: (b, s, 0)),
                  pl.BlockSpec((CONV_W, d), full), pl.BlockSpec((1, d), full),
                  pl.BlockSpec((d, d), full), pl.BlockSpec((1, d), full),
                  pl.BlockSpec((d, d), full), pl.BlockSpec((1, d), full), pl.BlockSpec((1, d), full)],
        out_specs=pl.BlockSpec((1, rows, d), lambda b, s: (b, s, 0)),
        out_shape=jax.ShapeDtypeStruct((B, S, d), BF16),
        scratch_shapes=[pltpu.VMEM((SUBLANES, d), F32), pltpu.VMEM((SUBLANES, d), F32)],
        compiler_params=_cparams("arbitrary", "arbitrary"),
        name="rglru",
    )(rg3, conv_w, conv_b, wa_bd, b_a, wx_bd, b_x, lam)


def _compress_kernel(x_ref, pos_ref, w1a_ref, w1b_ref, b1_ref, w2_ref, b2_ref, o_ref, *, n_chunks):
    x = x_ref[0, 0, 0]
    w1a = w1a_ref[0]
    w1b = w1b_ref[0]
    pos = pos_ref[0]
    half = x.shape[1]
    pos_term = _dot(pos[:, :half], w1a) + _dot(pos[:, half:], w1b)
    first = _dot(x, w1a)
    second = _dot(x, w1b)
    hidden = first + pltpu.roll(second, n_chunks - 1, 0) + pos_term[0:1, :] + b1_ref[0]
    out = _dot(_gelu_tanh(hidden).astype(BF16), w2_ref[0]) + b2_ref[0]
    rows = lax.broadcasted_iota(jnp.int32, out.shape, 0)
    o_ref[0, 0, 0] = jnp.where(rows < n_chunks - 1, out, 0.0).astype(o_ref.dtype)


def _compress(chunks, pos, w1a, w1b, b1, w2, b2):
    two, HKV, B, NCH, half = chunks.shape
    hid = w1a.shape[-1]
    dh = w2.shape[-1]
    kv = lambda c, h, b: (c, 0, 0)
    return pl.pallas_call(
        functools.partial(_compress_kernel, n_chunks=NCH),
        grid=(two, HKV, B),
        in_specs=[pl.BlockSpec((1, 1, 1, NCH, half), lambda c, h, b: (c, h, b, 0, 0)),
                  pl.BlockSpec((1, SUBLANES, 2 * half), kv),
                  pl.BlockSpec((1, half, hid), kv), pl.BlockSpec((1, half, hid), kv),
                  pl.BlockSpec((1, 1, hid), kv), pl.BlockSpec((1, hid, dh), kv), pl.BlockSpec((1, 1, dh), kv)],
        out_specs=pl.BlockSpec((1, 1, 1, NCH, dh), lambda c, h, b: (c, h, b, 0, 0)),
        out_shape=jax.ShapeDtypeStruct((two, HKV, B, NCH, dh), BF16),
        compiler_params=_cparams("arbitrary", "arbitrary", "arbitrary"),
        name="compress_kv",
    )(chunks, pos, w1a, w1b, b1, w2, b2)


def _t5_bucket_np(dist):
    n = np.maximum(dist, 0)
    max_exact = NUM_BUCKETS // 2
    nf = np.maximum(n, 1).astype(np.float32)
    large = max_exact + (np.log(nf / np.float32(max_exact)) / np.float32(math.log(MAX_DISTANCE / max_exact))
                         * np.float32(NUM_BUCKETS - max_exact)).astype(np.int32)
    return np.where(n < max_exact, n, np.minimum(large, NUM_BUCKETS - 1)).astype(np.int32)


def _bias_table(rel_bias, offset, n_rows, lo, hi, row_step=1):
    R = n_rows * row_step
    P = R + Q_BLOCK - 1
    d = np.arange(P) + (offset - (R - 1))
    f = rel_bias.astype(F32)[jnp.asarray(_t5_bucket_np(d))]
    f = jnp.where(jnp.asarray((d >= lo) & (d < hi))[:, None], f, NEG_INF).T
    hankel = jnp.tile(f, (1, R + 1))[:, :R * (P + 1)].reshape(N_HEADS, R, P + 1)[:, :, :Q_BLOCK]
    tab = hankel[:, ::-1][:, ::row_step]
    return (tab.reshape(N_KV_HEADS, GQA, n_rows, Q_BLOCK).transpose(0, 2, 1, 3)
            .reshape(N_KV_HEADS, n_rows, GQA * Q_BLOCK))


def _dot_tn(a, b):
    return lax.dot_general(a, b, (((0,), (0,)), ((), ())), preferred_element_type=F32)


def _nsa_kernel(qt_ref, kcp_ref, vcp_ref, vct_ref, ov_ref, ovt_ref, ka_ref, vst_ref, kw_ref, vwt_ref, gate_ref,
                b31_ref, b31p_ref, bcn_ref, bw_ref, bsn_ref, o_ref, qa_scr, m_scr, acc_scr, ow_scr,
                sa_scr, sb_scr, *, dh):
    qb = pl.program_id(2)
    q0 = qb * Q_BLOCK
    cols = GQA * Q_BLOCK
    qt = qt_ref[0, 0, 0]
    b31 = b31_ref[0]
    lane_q = lax.broadcasted_iota(jnp.int32, (1, cols), 1) % Q_BLOCK

    n_cp = kcp_ref.shape[2]
    cpb = Q_BLOCK // CMP_STRIDE
    s_far = _dot(kcp_ref[0, 0], qt) + b31
    row = lax.broadcasted_iota(jnp.int32, (n_cp, cols), 0)
    s_far = jnp.where((row >= CMP_PAD) & (row < cpb * qb), s_far, NEG_INF)
    near0 = pl.multiple_of(cpb * qb, SUBLANES)
    s_near = _dot(kcp_ref[0, 0, pl.ds(near0, CMP_NEAR), :], qt) + bcn_ref[0]
    rown = lax.broadcasted_iota(jnp.int32, (CMP_NEAR, cols), 0)
    s_near = jnp.where(rown + cpb * qb >= CMP_PAD, s_near, NEG_INF)
    m_c = jnp.maximum(jnp.max(s_far, axis=0, keepdims=True), jnp.max(s_near, axis=0, keepdims=True))
    p_far = jnp.exp(s_far - m_c)
    p_near = jnp.exp(s_near - m_c)
    l_c = jnp.sum(p_far, axis=0, keepdims=True) + jnp.sum(p_near, axis=0, keepdims=True)
    any_valid = (q0 + lane_q >= CMP_LEN - 1).astype(F32)
    scale_c = any_valid / l_c
    pf_hi = p_far.astype(BF16)
    pn_hi = p_near.astype(BF16)
    vc_near = vcp_ref[0, 0, pl.ds(near0, CMP_NEAR), :]
    o_c = (_dot(vct_ref[0, 0], pf_hi) + _dot_tn(vc_near, pn_hi)) * scale_c
    pf_lo = (p_far - pf_hi.astype(F32)).astype(BF16)
    pn_lo = (p_near - pn_hi.astype(F32)).astype(BF16)
    ov_near = ov_ref[pl.ds(near0, CMP_NEAR), :]
    imp4 = (_dot(ovt_ref[...], pf_hi) + _dot(ovt_ref[...], pf_lo)
            + _dot_tn(ov_near, pn_hi) + _dot_tn(ov_near, pn_lo)) * scale_c
    imp = imp4[:, 0:Q_BLOCK]
    for g in range(1, GQA):
        imp = imp + imp4[:, g * Q_BLOCK:(g + 1) * Q_BLOCK]

    n_win = WINDOW + Q_BLOCK
    ks0 = pl.multiple_of(q0, Q_BLOCK)
    s_w = _dot(kw_ref[0, 0, pl.ds(ks0, n_win), :], qt) + bw_ref[0]
    roww = lax.broadcasted_iota(jnp.int32, (n_win, cols), 0)
    s_w = jnp.where(roww + q0 >= WINDOW, s_w, NEG_INF)
    p_w = jnp.exp(s_w - jnp.max(s_w, axis=0, keepdims=True)).astype(BF16)
    acc_w = _dot(vwt_ref[0, 0, :, pl.ds(ks0, n_win)], p_w)
    gates = gate_ref[0, 0, 0]
    ow_scr[...] = gates[0:1, :] * o_c + gates[2:3, :] * (acc_w[0:dh, :] / acc_w[dh:dh + 1, :])

    n_sb = imp.shape[0]
    blk = lax.broadcasted_iota(jnp.int32, (n_sb, Q_BLOCK), 0)
    qq = lax.broadcasted_iota(jnp.int32, (n_sb, Q_BLOCK), 1)
    cur = (q0 + qq) // SEL_BLOCK
    forced = (blk == 0) | (blk == cur) | (blk == cur - 1)
    v = jnp.where(forced, FORCE_SCORE, jnp.where(blk <= cur, imp, -1.0))
    sel = jnp.zeros((n_sb, Q_BLOCK), F32)
    for _ in range(min(SEL_TOPN, n_sb)):
        mx = jnp.max(v, axis=0, keepdims=True)
        idx = jnp.min(jnp.where(v == mx, blk, n_sb), axis=0, keepdims=True)
        hit = blk == idx
        sel = jnp.where(hit, 1.0, sel)
        v = jnp.where(hit, -jnp.inf, v)
    near_blk0 = 2 * qb - SEL_NEAR // SEL_BLOCK // 2
    neg_near = jnp.where(sel > 0.5, 0.0, NEG_INF)
    neg_far = jnp.where(blk < near_blk0, neg_near, NEG_INF).astype(BF16)
    neg_near = neg_near.astype(BF16)

    qa_scr[n_sb:n_sb + dh, :] = qt
    qa_scr[n_sb + dh:, :] = jnp.zeros((KA_WIDTH - n_sb - dh, cols), BF16)
    for g in range(GQA):
        qa_scr[:n_sb, g * Q_BLOCK:(g + 1) * Q_BLOCK] = neg_near
    s_n =_dot(ka_ref[0, 0, pl.ds(ks0, SEL_NEAR), :], qa_scr[...]) + bsn_ref[0]
    rowk = lax.broadcasted_iota(jnp.int32, (SEL_NEAR, cols), 0)
    s_n = jnp.where(rowk + q0 >= SEL_NEAR // 2, s_n, NEG_INF)
    m0 = jnp.max(s_n, axis=0, keepdims=True)
    m_scr[...] = m0
    acc_scr[...] = _dot(vst_ref[0, 0, :, pl.ds(ks0, SEL_NEAR)], jnp.exp(s_n - m0).astype(BF16))

    for g in range(GQA):
        qa_scr[:n_sb, g * Q_BLOCK:(g + 1) * Q_BLOCK] = neg_far
    qa_scr[n_sb + dh:n_sb + dh + BIAS_ROWS, :] = b31p_ref[0]
    n_far = jnp.maximum(q0 - SEL_NEAR // 2 + SEL_FAR_KEYS - 1, 0) // SEL_FAR_KEYS

    def far_scores(k0):
        return _dot(ka_ref[0, 0, pl.ds(k0, SEL_FAR_KEYS), :], qa_scr[...])

    def far_update(k0, s_f):
        m_old = m_scr[...]
        m_new = jnp.maximum(m_old, jnp.max(s_f, axis=0, keepdims=True))
        p_f = jnp.exp(s_f - m_new).astype(BF16)
        acc_scr[...] = (jnp.exp(m_old - m_new) * acc_scr[...]
                        + _dot(vst_ref[0, 0, :, pl.ds(k0, SEL_FAR_KEYS)], p_f))
        m_scr[...] = m_new

    def far_pair(t, carry):
        k0 = pl.multiple_of(SEL_NEAR // 2 + 2 * t * SEL_FAR_KEYS, SEL_NEAR // 2)
        k1 = pl.multiple_of(k0 + SEL_FAR_KEYS, SEL_NEAR // 2)
        k2 = pl.multiple_of(k1 + SEL_FAR_KEYS, SEL_NEAR // 2)
        sb_scr[...] = far_scores(k1)
        far_update(k0, sa_scr[...])
        sa_scr[...] = far_scores(k2)
        far_update(k1, sb_scr[...])
        return carry

    n_pairs = (n_far + 1) // 2

    @pl.when(n_pairs > 0)
    def _():
        sa_scr[...] = far_scores(SEL_NEAR // 2)

    lax.fori_loop(0, n_pairs, far_pair, 0)
    o_s = acc_scr[0:dh, :] / acc_scr[dh:dh + 1, :]

    o_ref[0, 0, 0] = (ow_scr[...] + gates[1:2, :] * o_s).astype(o_ref.dtype)


def _nsa(q_t, kc_pad, vc_pad, vc_t, ov_pad, ov_t, ka_pad, vs_t, kw_pad, vw_t, gates_t, b31, b31p, bcn, bw, bsn):
    B, HKV, NQB, dh, cols = q_t.shape
    per_bh = lambda b, h, i: (h, b, 0, 0)
    per_h = lambda b, h, i: (h, 0, 0)
    whole = lambda b, h, i: (0, 0)
    per_blk = lambda b, h, i: (b, h, i, 0, 0)
    bh_spec = lambda a: pl.BlockSpec((1, 1) + a.shape[2:], per_bh)
    h_spec = lambda a: pl.BlockSpec((1,) + a.shape[1:], per_h)
    return pl.pallas_call(
        functools.partial(_nsa_kernel, dh=dh),
        grid=(B, HKV, NQB),
        in_specs=[pl.BlockSpec((1, 1, 1, dh, cols), per_blk),
                  bh_spec(kc_pad), bh_spec(vc_pad), bh_spec(vc_t),
                  pl.BlockSpec(ov_pad.shape, whole), pl.BlockSpec(ov_t.shape, whole),
                  bh_spec(ka_pad), bh_spec(vs_t), bh_spec(kw_pad), bh_spec(vw_t),
                  pl.BlockSpec((1, 1, 1) + gates_t.shape[3:], per_blk),
                  h_spec(b31), h_spec(b31p), h_spec(bcn), h_spec(bw), h_spec(bsn)],
        out_specs=pl.BlockSpec((1, 1, 1, dh, cols), per_blk),
        out_shape=jax.ShapeDtypeStruct((B, HKV, NQB, dh, cols), BF16),
        scratch_shapes=[pltpu.VMEM((KA_WIDTH, cols), BF16), pltpu.VMEM((1, cols), F32),
                        pltpu.VMEM((vs_t.shape[2], cols), F32), pltpu.VMEM((dh, cols), F32),
                        pltpu.VMEM((SEL_FAR_KEYS, cols), F32), pltpu.VMEM((SEL_FAR_KEYS, cols), F32)],
        compiler_params=_cparams("arbitrary", "arbitrary", "arbitrary"),
        name="nsa_attention",
    )(q_t, kc_pad, vc_pad, vc_t, ov_pad, ov_t, ka_pad, vs_t, kw_pad, vw_t, gates_t, b31, b31p, bcn, bw, bsn)


def _attention(kv, q_t, gates_t, k_cmp, v_cmp, rel_bias, B, S, dh):
    HKV = N_KV_HEADS
    n_sb = S // SEL_BLOCK
    assert n_sb == LANES, "selection blocks are laid out on one lane tile"

    def kv_heads(j):
        return kv[:, :, j * dh:(j + 1) * dh].reshape(HKV, B, S, dh)

    def values_t(v, n_front, n_back):
        vt = jnp.concatenate([v.transpose(0, 1, 3, 2), jnp.ones((HKV, B, 1, S), BF16),
                              jnp.zeros((HKV, B, V_ROWS - dh - 1, S), BF16)], axis=2)
        return jnp.pad(vt, ((0, 0), (0, 0), (0, 0), (n_front, n_back)))

    ks, vs, kw, vw = kv_heads(2), kv_heads(3), kv_heads(4), kv_heads(5)
    front = lambda a, n: jnp.pad(a, ((0, 0), (0, 0), (n, 0), (0, 0)))
    feat = np.zeros((S + SEL_FAR_KEYS, n_sb + BIAS_ROWS), np.float32)
    feat[np.arange(S), np.arange(S) // SEL_BLOCK] = 1.0
    feat[S:, n_sb - 1] = 1.0
    feat[:S, n_sb:n_sb + BIAS_PIECES] = 1.0
    feat = jnp.asarray(feat, BF16)
    ks_back = jnp.pad(ks, ((0, 0), (0, 0), (0, SEL_FAR_KEYS), (0, 0)))
    rows_k = S + SEL_FAR_KEYS
    ka = jnp.concatenate([jnp.broadcast_to(feat[:, :n_sb], (HKV, B, rows_k, n_sb)), ks_back,
                          jnp.broadcast_to(feat[:, n_sb:], (HKV, B, rows_k, BIAS_ROWS)),
                          jnp.zeros((HKV, B, rows_k, KA_WIDTH - n_sb - dh - BIAS_ROWS), BF16)], axis=-1)
    ka_pad = front(ka, SEL_NEAR // 2)
    vs_t = values_t(vs, SEL_NEAR // 2, SEL_FAR_KEYS)
    kw_pad = front(kw, WINDOW)
    vw_t = values_t(vw, WINDOW, 0)
    kc_pad = front(k_cmp, CMP_PAD)
    vc_pad = front(v_cmp, CMP_PAD)
    vc_t = vc_pad.transpose(0, 1, 3, 2)
    n_c = S // CMP_STRIDE
    c_start = np.arange(n_c)[:, None] * CMP_STRIDE
    s_start = np.arange(n_sb)[None, :] * SEL_BLOCK
    overlap = ((c_start < s_start + SEL_BLOCK) & (c_start + CMP_LEN > s_start)).astype(np.float32)
    overlap[n_c - 1] = 0.0
    overlap = np.concatenate([np.zeros((CMP_PAD, n_sb), np.float32), overlap])
    ov_pad = jnp.asarray(overlap, BF16)
    ov_t = jnp.asarray(overlap.T.copy(), BF16)


    far = S + WINDOW
    bcn = _bias_table(rel_bias, CMP_STRIDE * CMP_PAD - (CMP_LEN - 1), CMP_NEAR, 0, far, row_step=CMP_STRIDE)
    bw = _bias_table(rel_bias, WINDOW, WINDOW + Q_BLOCK, 0, WINDOW)
    bsn = _bias_table(rel_bias, SEL_NEAR // 2, SEL_NEAR, 0, far)
    b31 = jnp.repeat(rel_bias.astype(F32)[NUM_BUCKETS - 1].reshape(HKV, GQA), Q_BLOCK, axis=1)[:, None, :]
    pieces, rest = [], b31
    for _ in range(BIAS_PIECES):
        piece = rest.astype(BF16)
        pieces.append(piece)
        rest = rest - piece.astype(F32)
    b31p = jnp.concatenate(pieces + [jnp.zeros((HKV, BIAS_ROWS - BIAS_PIECES, GQA * Q_BLOCK), BF16)], axis=1)

    return _nsa(q_t, kc_pad, vc_pad, vc_t, ov_pad, ov_t, ka_pad, vs_t, kw_pad, vw_t, gates_t, b31, b31p, bcn, bw, bsn)


def _outproj_kernel(h_ref, yr_ref, yat_ref, wr_ref, wa_ref, g_ref, b_ref, o_ref, *, dh):
    blocks = []
    for qbi in range(yat_ref.shape[2]):
        tiles = []
        for j, hk, g0 in _head_slabs():
            slabs = [yat_ref[0, hk, qbi, :, (g0 + u) * Q_BLOCK:(g0 + u + 1) * Q_BLOCK].astype(F32)
                     for u in range(HEADS_PER_TILE)]
            tiles.append(jnp.concatenate(slabs, axis=0).T)
        blocks.append(jnp.concatenate(tiles, axis=1))
    y_att = jnp.concatenate(blocks, axis=0).astype(BF16)
    mix = _dot(yr_ref[...], wr_ref[...]) + _dot(y_att, wa_ref[...])
    o_ref[...] = _layer_norm(ALPHA * h_ref[...] + mix, g_ref[...], b_ref[...])


def _outproj_ln(h, y_rnn, y_att_t, w_r, w_a, g, b):
    T, D = h.shape
    dr, da = y_rnn.shape[1], w_a.shape[0]
    B, HKV, nqb, dh, cols = y_att_t.shape
    tm = PROJ_ROWS
    per_b = (nqb * Q_BLOCK) // tm
    nq = tm // Q_BLOCK
    row = lambda i: (i, 0)
    full = lambda i: (0, 0)
    return pl.pallas_call(
        functools.partial(_outproj_kernel, dh=dh),
        grid=(T // tm,),
        in_specs=[pl.BlockSpec((tm, D), row), pl.BlockSpec((tm, dr), row),
                  pl.BlockSpec((1, HKV, nq, dh, cols), lambda i: (i // per_b, 0, i % per_b, 0, 0)),
                  pl.BlockSpec((dr, D), full), pl.BlockSpec((da, D), full),
                  pl.BlockSpec((1, D), full), pl.BlockSpec((1, D), full)],
        out_specs=pl.BlockSpec((tm, D), row),
        out_shape=jax.ShapeDtypeStruct((T, D), F32),
        compiler_params=_cparams("arbitrary"),
        name="outproj_ln",
    )(h, y_rnn, y_att_t, w_r, w_a, g, b)


def _route_kernel(h_ref, wh_ref, wl_ref, rb_ref, tri_ref, e_ref, p_ref, w_ref, cnt_ref, carry_scr, *, cols):
    i = pl.program_id(0)

    @pl.when(i == 0)
    def _():
        carry_scr[...] = jnp.zeros_like(carry_scr)

    h = h_ref[...]
    h_hi = h.astype(BF16)
    h_lo = (h - h_hi.astype(F32)).astype(BF16)
    logits = _dot_nt(wh_ref[...], h_hi) + (_dot_nt(wl_ref[...], h_hi) + _dot_nt(wh_ref[...], h_lo))
    scores = _sigmoid(logits)
    biased = scores + rb_ref[...]
    gsz = N_EXPERTS // N_GROUPS
    gscore = []
    for g in range(N_GROUPS):
        blk = biased[g * gsz:(g + 1) * gsz]
        m1 = jnp.max(blk, axis=0, keepdims=True)
        is_m1 = blk == m1
        n_m1 = jnp.sum(jnp.where(is_m1, 1.0, 0.0), axis=0, keepdims=True)
        m2 = jnp.max(jnp.where(is_m1, -jnp.inf, blk), axis=0, keepdims=True)
        gscore.append(m1 + jnp.where(n_m1 > 1.5, m1, m2))
    masked = []
    for g in range(N_GROUPS):
        rank = jnp.zeros_like(gscore[g])
        for o in range(N_GROUPS):
            if o == g:
                continue
            ahead = (gscore[o] > gscore[g]) | (gscore[o] == gscore[g]) if o < g else gscore[o] > gscore[g]
            rank = rank + jnp.where(ahead, 1.0, 0.0)
        keep = rank < TOPK_GROUPS - 0.5
        masked.append(jnp.where(keep, biased[g * gsz:(g + 1) * gsz], NEG_INF))
    v = jnp.concatenate(masked, axis=0)
    rows = lax.broadcasted_iota(jnp.int32, (N_EXPERTS, cols), 0)
    sel = jnp.zeros((N_EXPERTS, cols), F32)
    idxs, svals = [], []
    for _ in range(TOP_K):
        mx = jnp.max(v, axis=0, keepdims=True)
        idx = jnp.min(jnp.where(v == mx, rows, N_EXPERTS), axis=0, keepdims=True)
        hit = rows == idx
        idxs.append(idx)
        svals.append(jnp.sum(jnp.where(hit, scores, 0.0), axis=0, keepdims=True))
        sel = jnp.where(hit, 1.0, sel)
        v = jnp.where(hit, -jnp.inf, v)
    total = svals[0]
    for k in range(1, TOP_K):
        total = total + svals[k]
    prefix = _dot(sel.astype(BF16), tri_ref[...]) + carry_scr[:, 0:1]
    for k in range(TOP_K):
        hit = rows == idxs[k]
        e_ref[k:k + 1, :] = idxs[k]
        p_ref[k:k + 1, :] = jnp.sum(jnp.where(hit, prefix, 0.0), axis=0, keepdims=True).astype(jnp.int32)
        w_ref[k:k + 1, :] = svals[k] / total * ROUTED_SCALE
    new_carry = carry_scr[...] + jnp.sum(sel, axis=1, keepdims=True)
    carry_scr[...] = new_carry
    cnt_ref[...] = new_carry


def _route(h1, rw_hi, rw_lo, rb, tri):
    T, D = h1.shape
    cols = ROUTE_COLS
    full = lambda i: (0, 0)
    out_col = pl.BlockSpec((TOP_K, cols), lambda i: (0, i))
    return pl.pallas_call(
        functools.partial(_route_kernel, cols=cols),
        grid=(T // cols,),
        in_specs=[pl.BlockSpec((cols, D), lambda i: (i, 0)),
                  pl.BlockSpec((N_EXPERTS, D), full), pl.BlockSpec((N_EXPERTS, D), full),
                  pl.BlockSpec((N_EXPERTS, 1), full), pl.BlockSpec((cols, cols), full)],
        out_specs=[out_col, out_col, out_col, pl.BlockSpec((N_EXPERTS, LANES), full)],
        out_shape=[jax.ShapeDtypeStruct((TOP_K, T), jnp.int32), jax.ShapeDtypeStruct((TOP_K, T), jnp.int32),
                   jax.ShapeDtypeStruct((TOP_K, T), F32), jax.ShapeDtypeStruct((N_EXPERTS, LANES), F32)],
        scratch_shapes=[pltpu.VMEM((N_EXPERTS, LANES), F32)],
        compiler_params=_cparams("arbitrary"),
        name="router",
    )(h1, rw_hi, rw_lo, rb, tri)


def _tile_major(a, tile):
    k, t = a.shape
    return a.reshape(k, t // tile, tile).transpose(1, 0, 2).reshape(-1)


def _dest_kernel(pstart_ref, e_ref, p_ref, d_ref):
    e = e_ref[...]

    def add_start(x, acc):
        return acc + jnp.where(e == x, pstart_ref[x], 0)

    d_ref[...] = lax.fori_loop(0, N_EXPERTS, add_start, p_ref[...])


def _dest_rows(eidx, pos, pstart):
    k, T = eidx.shape
    blk = pl.BlockSpec((k, DEST_COLS), lambda i, ps: (0, i))
    return pl.pallas_call(
        _dest_kernel,
        grid_spec=pltpu.PrefetchScalarGridSpec(num_scalar_prefetch=1, grid=(T // DEST_COLS,),
                                               in_specs=[blk, blk], out_specs=blk),
        out_shape=jax.ShapeDtypeStruct((k, T), jnp.int32),
        compiler_params=_cparams("arbitrary"),
        name="moe_dest_rows",
    )(pstart, eidx, pos)


def _load_indices(d_hbm, d_smem, idx_sem, n, step):
    base = pl.multiple_of(step * n, n)
    cp = pltpu.make_async_copy(d_hbm.at[pl.ds(base, n)], d_smem, idx_sem)
    cp.start()
    cp.wait()


def _issue_rows(n, make_copy):
    def body(c, carry):
        for u in range(ISSUE_UNROLL):
            make_copy(c * ISSUE_UNROLL + u).start(priority=u % 2)
        return carry

    lax.fori_loop(0, n // ISSUE_UNROLL, body, 0)


def _token_tile(ref, t):
    return ref.at[pl.ds(pl.multiple_of(t * SUBLANES, SUBLANES), SUBLANES), :]


def _to_token_tiles(dst_ref, x, n):
    for c in range(SUBLANES):
        dst_ref[pl.ds(c, n, stride=SUBLANES), :] = x[:, c * LANES:(c + 1) * LANES]


def _from_token_tiles(src_ref, first, n, c):
    return src_ref[pl.ds(first * SUBLANES + c, n, stride=SUBLANES), :]


def _dispatch_kernel(h_ref, d_hbm, xs_out, d_smem, stage, idx_sem, row_sem, *, rows):
    n = TOP_K * rows
    _load_indices(d_hbm, d_smem, idx_sem, n, pl.program_id(0))
    _to_token_tiles(stage, h_ref[...], rows)

    def row_copy(j):
        return pltpu.make_async_copy(_token_tile(stage, j & (rows - 1)), _token_tile(xs_out, d_smem[j]), row_sem)

    _issue_rows(n, row_copy)
    whole = xs_out.at[pl.ds(0, n * SUBLANES), :]
    pltpu.make_async_copy(whole, whole, row_sem).wait()


def _dispatch(h1, d_flat, n_rows):
    T, D = h1.shape
    rows = DISPATCH_ROWS
    n = TOP_K * rows
    return pl.pallas_call(
        functools.partial(_dispatch_kernel, rows=rows),
        grid=(T // rows,),
        in_specs=[pl.BlockSpec((rows, D), lambda i: (i, 0)), pl.BlockSpec(memory_space=pl.ANY)],
        out_specs=pl.BlockSpec(memory_space=pl.ANY),
        scratch_shapes=[pltpu.SMEM((n,), jnp.int32), pltpu.VMEM((rows * SUBLANES, LANES), F32),
                        pltpu.SemaphoreType.DMA(()), pltpu.SemaphoreType.DMA(())],
        out_shape=jax.ShapeDtypeStruct((n_rows * SUBLANES, LANES), F32),
        compiler_params=_cparams("arbitrary"),
        name="moe_dispatch",
    )(h1, d_flat)


def _expert_kernel(be_ref, nv_ref, na_ref, x_ref, wg_ref, wu_ref, wd_ref, y_ref, xb, wgb, wub, wdb):
    del na_ref
    i = pl.program_id(0)
    bm = x_ref.shape[0] // SUBLANES

    @pl.when((i == 0) | (be_ref[i] != be_ref[jnp.maximum(i - 1, 0)]))
    def _():
        wgb[...] = wg_ref[0].astype(BF16)
        wub[...] = wu_ref[0].astype(BF16)
        wdb[...] = wd_ref[0].astype(BF16)

    @pl.when(nv_ref[i] > 0)
    def _():
        valid = lax.broadcasted_iota(jnp.int32, (bm, LANES), 0) < nv_ref[i]
        for c in range(SUBLANES):
            x_c = _from_token_tiles(x_ref, 0, bm, c)
            xb[:, c * LANES:(c + 1) * LANES] = jnp.where(valid, x_c, 0.0).astype(BF16)
        x = xb[...]
        g = _dot(x, wgb[...])
        u = _dot(x, wub[...])
        hmid = (g * _sigmoid(g)) * u
        _to_token_tiles(y_ref, _dot(hmid.astype(BF16), wdb[...]), bm)

    @pl.when(nv_ref[i] <= 0)
    def _():
        y_ref[...] = jnp.zeros_like(y_ref)


def _experts(xs, w_gate, w_up, w_down, block_e, n_valid, n_active):
    R = xs.shape[0] // SUBLANES
    E, D, de = w_gate.shape
    bm = EXPERT_ROWS
    x_map = lambda i, be, nv, na: (jnp.minimum(i, jnp.maximum(na[0] - 1, 0)), 0)
    w_map = lambda i, be, nv, na: (be[i], 0, 0)
    return pl.pallas_call(
        _expert_kernel,
        grid_spec=pltpu.PrefetchScalarGridSpec(
            num_scalar_prefetch=3,
            grid=(R // bm,),
            in_specs=[pl.BlockSpec((bm * SUBLANES, LANES), x_map), pl.BlockSpec((1, D, de), w_map),
                      pl.BlockSpec((1, D, de), w_map), pl.BlockSpec((1, de, D), w_map)],
            out_specs=pl.BlockSpec((bm * SUBLANES, LANES), lambda i, be, nv, na: (i, 0)),
            scratch_shapes=[pltpu.VMEM((bm, D), BF16), pltpu.VMEM((D, de), BF16),
                            pltpu.VMEM((D, de), BF16), pltpu.VMEM((de, D), BF16)]),
        out_shape=jax.ShapeDtypeStruct((R * SUBLANES, LANES), F32),
        compiler_params=_cparams("arbitrary"),
        name="moe_experts",
    )(block_e, n_valid, n_active, xs, w_gate, w_up, w_down)


def _combine_kernel(h_ref, wk_ref, d_hbm, ys_hbm, sg_ref, su_ref, sd_ref, g_ref, b_ref,
                    o_ref, d_smem, buf, idx_sem, row_sem, *, rows):
    i = pl.program_id(0)
    n = TOP_K * rows
    slot = lax.rem(i, 2)

    def fetch(step, into):
        _load_indices(d_hbm, d_smem, idx_sem, n, step)

        def row_copy(j):
            return pltpu.make_async_copy(_token_tile(ys_hbm, d_smem[j]), _token_tile(buf.at[into], j),
                                         row_sem.at[into])

        _issue_rows(n, row_copy)

    @pl.when(i == 0)
    def _():
        fetch(0, 0)

    @pl.when(i + 1 < pl.num_programs(0))
    def _():
        fetch(i + 1, 1 - slot)

    h = h_ref[...]
    hb = h.astype(BF16)
    g = _dot(hb, sg_ref[...])
    u = _dot(hb, su_ref[...])
    shared = _dot(((g * _sigmoid(g)) * u).astype(BF16), sd_ref[...])
    cur = buf.at[slot]
    pltpu.make_async_copy(ys_hbm.at[pl.ds(0, n * SUBLANES), :], cur, row_sem.at[slot]).wait()
    wk = wk_ref[...]
    cols = []
    for c in range(SUBLANES):
        acc = wk[:, 0:1] * _from_token_tiles(cur, 0, rows, c)
        for k in range(1, TOP_K):
            acc = acc + wk[:, k:k + 1] * _from_token_tiles(cur, k * rows, rows, c)
        cols.append(acc)
    routed = jnp.concatenate(cols, axis=1)
    o_ref[...] = _layer_norm(ALPHA * h + (routed + shared), g_ref[...], b_ref[...])


def _combine(h1, wk_t, d_flat, ys, sh_gate, sh_up, sh_down, g, b):
    T, D = h1.shape
    ds = sh_gate.shape[1]
    rows = COMBINE_ROWS
    n = TOP_K * rows
    full = lambda i: (0, 0)
    return pl.pallas_call(
        functools.partial(_combine_kernel, rows=rows),
        grid=(T // rows,),
        in_specs=[pl.BlockSpec((rows, D), lambda i: (i, 0)), pl.BlockSpec((rows, TOP_K), lambda i: (i, 0)),
                  pl.BlockSpec(memory_space=pl.ANY), pl.BlockSpec(memory_space=pl.ANY),
                  pl.BlockSpec((D, ds), full), pl.BlockSpec((D, ds), full), pl.BlockSpec((ds, D), full),
                  pl.BlockSpec((1, D), full), pl.BlockSpec((1, D), full)],
        out_specs=pl.BlockSpec((rows, D), lambda i: (i, 0)),
        scratch_shapes=[pltpu.SMEM((n,), jnp.int32), pltpu.VMEM((2, n * SUBLANES, LANES), F32),
                        pltpu.SemaphoreType.DMA(()), pltpu.SemaphoreType.DMA((2,))],
        out_shape=jax.ShapeDtypeStruct((T, D), F32),
        compiler_params=_cparams("arbitrary"),
        name="moe_combine",
    )(h1, wk_t, d_flat, ys, sh_gate, sh_up, sh_down, g, b)


def _moe(h1, router_w, router_b, w_gate, w_up, w_down, sh_gate, sh_up, sh_down, ln_g, ln_b):
    T, D = h1.shape
    assert D == SUBLANES * LANES, "a token row is moved as one (SUBLANES, LANES) tile"
    rw_t = router_w.T
    rw_hi = rw_t.astype(BF16)
    rw_lo = (rw_t - rw_hi.astype(F32)).astype(BF16)
    tri = jnp.asarray(np.triu(np.ones((ROUTE_COLS, ROUTE_COLS), np.float32), 1), BF16)
    eidx, pos, wk, cnt = _route(h1, rw_hi, rw_lo, router_b.reshape(N_EXPERTS, 1), tri)

    bm = EXPERT_ROWS
    counts = cnt[:, 0].astype(jnp.int32)
    padded = (counts + bm - 1) // bm * bm
    pend = jnp.cumsum(padded)
    pstart = (pend - padded).astype(jnp.int32)
    n_blocks = T * TOP_K // bm + N_EXPERTS
    n_active = (pend[-1:] // bm).astype(jnp.int32)
    block_row0 = jnp.arange(n_blocks, dtype=jnp.int32) * bm
    block_e = jnp.minimum(jnp.sum((pend[None, :] <= block_row0[:, None]).astype(jnp.int32), axis=1), N_EXPERTS - 1)
    is_e = block_e[:, None] == jnp.arange(N_EXPERTS, dtype=jnp.int32)[None, :]
    pick = lambda v: jnp.sum(jnp.where(is_e, v[None, :], 0), axis=1)
    n_valid = jnp.clip(pick(counts) - (block_row0 - pick(pstart)), 0, bm)
    dest = _dest_rows(eidx, pos, pstart)
    xs = _dispatch(h1, _tile_major(dest, DISPATCH_ROWS), n_blocks * bm)
    ys = _experts(xs, w_gate, w_up, w_down, block_e, n_valid.astype(jnp.int32), n_active)
    return _combine(h1, wk.T, _tile_major(dest, COMBINE_ROWS), ys,
                    sh_gate.astype(BF16), sh_up.astype(BF16), sh_down.astype(BF16),
                    ln_g.reshape(1, D), ln_b.reshape(1, D))


def _block_diag(w):
    n, c, d = w.shape
    eye = jnp.eye(n, dtype=w.dtype)
    return (eye[:, None, :, None] * w[:, :, None, :]).reshape(n * c, n * d)


def _mixer_and_ln(h_in_x, ln_g, ln_b, w_in, conv_w, conv_b, rg_w_a, rg_b_a, rg_w_x, rg_b_x, rg_lambda,
                  cmp_k, cmp_v, rel_bias, w_out, ln1_g, ln1_b, B, S, first):
    T, D = h_in_x.shape
    d_rnn = D // D_RNN_FRAC
    dh = (D - d_rnn) // N_HEADS
    kvw = N_KV_HEADS * dh
    n_att = N_HEADS * dh + 6 * kvw
    c_rg, c_att = 2 * d_rnn, 2 * d_rnn + n_att
    w_rg = w_in[:, :c_rg].astype(BF16)
    c_q = c_rg + N_HEADS * dh
    w_kv = w_in[:, c_q:c_att].reshape(D, 6, N_KV_HEADS, dh).transpose(0, 2, 1, 3).reshape(D, 6 * kvw)
    w_att = jnp.concatenate([w_in[:, c_rg:c_q], w_kv], axis=1).astype(BF16)
    w_gl = jnp.pad(w_in[:, c_att:], ((0, 0), (0, LANES - 3 * N_HEADS))).astype(BF16)
    if first:
        g0, b0 = ln_g.reshape(1, D), ln_b.reshape(1, D)
    else:
        raise NotImplementedError("depth > 1")
    assert HEADS_PER_TILE * dh == LANES and PROJ_ROWS % Q_BLOCK == 0 and S % PROJ_ROWS == 0
    h, rg, kv, q_t, gates_t = _ln_inproj(h_in_x, g0, b0, w_rg, w_att, w_gl, q_cols=N_HEADS * dh,
                                         q_scale=dh ** -0.5, dh=dh, B=B, S=S)

    y_rnn = _rglru(rg.reshape(B, S, c_rg), conv_w, conv_b.reshape(1, d_rnn),
                   _block_diag(rg_w_a).astype(BF16), rg_b_a.reshape(1, d_rnn),
                   _block_diag(rg_w_x).astype(BF16), rg_b_x.reshape(1, d_rnn), rg_lambda.reshape(1, d_rnn))

    nch = S // CMP_STRIDE

    def chunks_of(j):
        return kv[:, :, j * dh:(j + 1) * dh].reshape(N_KV_HEADS, B, nch, CMP_STRIDE * dh)

    chunks = jnp.stack([chunks_of(0), chunks_of(1)])
    half = CMP_STRIDE * dh
    stack = lambda i: jnp.stack([cmp_k[i], cmp_v[i]])
    pos = jnp.broadcast_to(stack(0).reshape(2, 1, CMP_LEN * dh), (2, SUBLANES, CMP_LEN * dh)).astype(BF16)
    w1 = stack(1).astype(BF16)
    kv_cmp = _compress(chunks, pos, w1[:, :half], w1[:, half:], stack(2)[:, None, :],
                       stack(3).astype(BF16), stack(4)[:, None, :])

    y_att = _attention(kv, q_t, gates_t, kv_cmp[0], kv_cmp[1], rel_bias, B, S, dh)
    w_o = w_out.astype(BF16)
    return _outproj_ln(h, y_rnn.reshape(T, d_rnn), y_att, w_o[:d_rnn], w_o[d_rnn:],
                       ln1_g.reshape(1, D), ln1_b.reshape(1, D))


def kernel(x, ln_in_g, ln_in_b, w_in, conv_w, conv_b, rg_w_a, rg_b_a, rg_w_x, rg_b_x, rg_lambda, cmp_pos_k, cmp_k_w1, cmp_k_b1, cmp_k_w2, cmp_k_b2, cmp_pos_v, cmp_v_w1, cmp_v_b1, cmp_v_w2, cmp_v_b2, rel_bias, w_out, ln1_g, ln1_b, router_w, router_b, w_gate, w_up, w_down, sh_gate, sh_up, sh_down, ln2_g, ln2_b):
    B, S, D = x.shape
    assert w_in.shape[0] == DEPTH
    l = 0
    cmp_k = (cmp_pos_k[l], cmp_k_w1[l], cmp_k_b1[l], cmp_k_w2[l], cmp_k_b2[l])
    cmp_v = (cmp_pos_v[l], cmp_v_w1[l], cmp_v_b1[l], cmp_v_w2[l], cmp_v_b2[l])
    h1 = _mixer_and_ln(x.reshape(B * S, D), ln_in_g, ln_in_b, w_in[l], conv_w[l], conv_b[l], rg_w_a[l], rg_b_a[l],
                       rg_w_x[l], rg_b_x[l], rg_lambda[l], cmp_k, cmp_v, rel_bias, w_out[l], ln1_g[l], ln1_b[l],
                       B, S, True)
    out = _moe(h1, router_w[l], router_b[l], w_gate[l], w_up[l], w_down[l], sh_gate[l], sh_up[l], sh_down[l],
               ln2_g[l], ln2_b[l])
    return out.reshape(B, S, D)
```
